```python
import math
import numpy as np
import jax, jax.numpy as jnp
from jax import lax

D_MODEL = 2048
BATCH = 2
SEQ = 8192
DEPTH = 4

GRID_W = 64
CTX_LEN = 256
HEAD_DIM = 128
DN_HEADS = 8
DN_WIDTH = DN_HEADS * HEAD_DIM
CONV_K = 5
CHUNK = 64
DT_MIN = 0.001
DT_MAX = 0.1
ATTN_Q_HEADS = 4
ATTN_KV_HEADS = 2
ATTN_WIDTH = ATTN_Q_HEADS * HEAD_DIM
ATTN_KV_WIDTH = 2 * ATTN_KV_HEADS * HEAD_DIM
Q_BLOCK = 128
ROPE_THETA = 10000.0
FNET_GROUPS = 4
FNET_GROUP_DIM = 128
FNET_WIDTH = FNET_GROUPS * FNET_GROUP_DIM
MIX_WIDTH = DN_WIDTH + ATTN_WIDTH + FNET_WIDTH
IN_SPLITS = (3 * DN_WIDTH, DN_WIDTH, 2 * DN_HEADS, 2 * DN_HEADS, ATTN_WIDTH, ATTN_KV_WIDTH, FNET_WIDTH)
IN_COLS = 4 * DN_WIDTH + 4 * DN_HEADS + ATTN_WIDTH + ATTN_KV_WIDTH + FNET_WIDTH
D_FF = 5632
N_EXPERTS = 8
TOP_K = 2
N_DENSE = (DEPTH + 1) // 2
N_MOE = DEPTH // 2
DEEPNORM_ALPHA = (2 * DEPTH) ** 0.25
DEEPNORM_BETA = (8 * DEPTH) ** -0.25
EPS = 1e-6

kernel_name = "hybrid_dit_deltanet_axialgqa_fnet_moe"


def _layer_norm(x, gain=None, bias=None):
    xf = x.astype(jnp.float32)
    mu = jnp.mean(xf, axis=-1, keepdims=True)
    var = jnp.mean(jnp.square(xf - mu), axis=-1, keepdims=True)
    y = (xf - mu) * lax.rsqrt(var + EPS)
    if gain is not None:
        y = y * gain.astype(jnp.float32) + bias.astype(jnp.float32)
    return y.astype(x.dtype)


def _modulate(x, shift, scale):
    return _layer_norm(x) * (1.0 + scale) + shift


def _rms_norm(x, w):
    xf = x.astype(jnp.float32)
    y = xf * lax.rsqrt(jnp.mean(jnp.square(xf), axis=-1, keepdims=True) + EPS)
    return (y * w.astype(jnp.float32)).astype(x.dtype)


def _l2norm(x):
    return x * lax.rsqrt(jnp.sum(jnp.square(x), axis=-1, keepdims=True) + EPS)


def _split_in(p):
    return jnp.split(p, np.cumsum(IN_SPLITS)[:-1].tolist(), axis=-1)


def _short_conv(u, w):
    pad = CONV_K // 2
    l = u.shape[1]
    up = jnp.pad(u, ((0, 0), (pad, pad), (0, 0)))
    y = up[:, 0:l] * w[0]
    for tap in range(1, CONV_K):
        y = y + up[:, tap:tap + l] * w[tap]
    return jax.nn.silu(y)


def _gated_delta_chunked(q, k, v, g, beta, state0):
    b, h, l, dk = q.shape
    dv = v.shape[-1]
    n = l // CHUNK
    q = q * dk ** -0.5
    k_beta = k * beta[..., None]
    v_beta = v * beta[..., None]
    rs = lambda t: t.reshape(b, h, n, CHUNK, t.shape[-1])
    q, k, k_beta, v_beta = rs(q), rs(k), rs(k_beta), rs(v_beta)
    g = jnp.cumsum(g.reshape(b, h, n, CHUNK), axis=-1)
    tril = jnp.tril(jnp.ones((CHUNK, CHUNK), bool))
    strict = jnp.tril(jnp.ones((CHUNK, CHUNK), bool), -1)
    decay = jnp.exp(jnp.where(tril, g[..., :, None] - g[..., None, :], -jnp.inf))
    lmat = jnp.where(strict, jnp.einsum('bhnid,bhnjd->bhnij', k_beta, k) * decay, 0.0)
    rhs = jnp.concatenate([v_beta, k_beta * jnp.exp(g)[..., None]], axis=-1)
    sol = lax.linalg.triangular_solve(lmat, rhs, left_side=True, lower=True, unit_diagonal=True)
    u, w = sol[..., :dv], sol[..., dv:]
    qk = jnp.where(tril, jnp.einsum('bhnid,bhnjd->bhnij', q, k) * decay, 0.0)

    def step(s, inp):
        q_c, k_c, u_c, w_c, g_c, qk_c = inp
        v_new = u_c - jnp.einsum('bhcd,bhde->bhce', w_c, s)
        o = (jnp.einsum('bhcd,bhde->bhce', q_c * jnp.exp(g_c)[..., None], s)
             + jnp.einsum('bhij,bhje->bhie', qk_c, v_new))
        g_last = g_c[..., -1]
        s = (s * jnp.exp(g_last)[..., None, None]
             + jnp.einsum('bhcd,bhce->bhde', k_c * jnp.exp(g_last[..., None] - g_c)[..., None], v_new))
        return s, o

    xs = tuple(jnp.moveaxis(t, 2, 0) for t in (q, k, u, w, g, qk))
    s_final, o = lax.scan(step, state0, xs)
    o = jnp.moveaxis(o, 0, 2).reshape(b, h, l, dv)
    return o, s_final


def _deltanet_prep(qkv_raw, beta_raw, a_raw, conv_w, a_log, dt_bias):
    b, l, _ = qkv_raw.shape
    qkv = _short_conv(qkv_raw, conv_w).astype(jnp.float32).reshape(b, l, 3, DN_HEADS, HEAD_DIM)
    qkv = jnp.transpose(qkv, (2, 0, 3, 1, 4))
    q, k, v = _l2norm(qkv[0]), _l2norm(qkv[1]), qkv[2]
    beta = jnp.transpose(jax.nn.sigmoid(beta_raw.astype(jnp.float32)).reshape(b, l, 2, DN_HEADS), (2, 0, 3, 1))
    dt = jax.nn.softplus(a_raw.astype(jnp.float32).reshape(b, l, 2, DN_HEADS) + dt_bias.astype(jnp.float32))
    g = -jnp.exp(a_log.astype(jnp.float32))[:, None, :, None] * jnp.transpose(dt, (2, 0, 3, 1))
    return q, k, v, beta, g


def _gated_rms_out(o, gate_raw, norm_w):
    b, h, l, d = o.shape
    o = jnp.transpose(o, (0, 2, 1, 3))
    y = o * lax.rsqrt(jnp.mean(jnp.square(o), axis=-1, keepdims=True) + EPS) * norm_w.astype(jnp.float32)
    y = y * jax.nn.silu(gate_raw.astype(jnp.float32).reshape(b, l, h, d))
    return y.reshape(b, l, h * d).astype(gate_raw.dtype)


def _deltanet(lat, con, conv_w, a_log, dt_bias, norm_w, need_ctx):
    qc, kc, vc, bc, gc = _deltanet_prep(con[0], con[2], con[3], conv_w, a_log, dt_bias)
    ql, kl, vl, bl, gl = _deltanet_prep(lat[0], lat[2], lat[3], conv_w, a_log, dt_bias)
    s0 = jnp.zeros(qc.shape[:2] + (HEAD_DIM, HEAD_DIM), jnp.float32)
    flip = lambda t: jnp.flip(t, axis=2)
    oc_f, s_f = _gated_delta_chunked(qc, kc, vc, gc[0], bc[0], s0)
    ol_f, _ = _gated_delta_chunked(ql, kl, vl, gl[0], bl[0], s_f)
    oc_b, s_b = _gated_delta_chunked(flip(qc), flip(kc), flip(vc), flip(gc[1]), flip(bc[1]), s0)
    ol_b, _ = _gated_delta_chunked(flip(ql), flip(kl), flip(vl), flip(gl[1]), flip(bl[1]), s_b)
    out_l = _gated_rms_out(ol_f + flip(ol_b), lat[1], norm_w)
    out_c = _gated_rms_out(oc_f + flip(oc_b), con[1], norm_w) if need_ctx else None
    return out_l, out_c


def _axial_rope_tables(rows):
    axis_dim = HEAD_DIM // 2
    inv = ROPE_THETA ** (-jnp.arange(0, axis_dim, 2, dtype=jnp.float32) / axis_dim)
    row = jnp.repeat(jnp.arange(rows, dtype=jnp.float32), GRID_W)
    col = jnp.tile(jnp.arange(GRID_W, dtype=jnp.float32), rows)
    ang = jnp.concatenate([row[:, None] * inv, col[:, None] * inv], axis=-1)
    return jnp.cos(ang), jnp.sin(ang)


def _apply_axial_rope(x, cos, sin):
    b, l, h, d = x.shape
    xa = x.astype(jnp.float32).reshape(b, l, h, 2, 2, d // 4)
    x1, x2 = xa[..., 0, :], xa[..., 1, :]
    cs = cos.reshape(l, 1, 2, d // 4)
    sn = sin.reshape(l, 1, 2, d // 4)
    out = jnp.stack([x1 * cs - x2 * sn, x1 * sn + x2 * cs], axis=-2)
    return out.reshape(b, l, h, d).astype(x.dtype)


def _attn_prep(q_raw, kv_raw, q_norm_w, k_norm_w):
    b, l, _ = q_raw.shape
    q = _rms_norm(q_raw.reshape(b, l, ATTN_Q_HEADS, HEAD_DIM), q_norm_w)
    kv = kv_raw.reshape(b, l, 2, ATTN_KV_HEADS, HEAD_DIM)
    k = _rms_norm(kv[:, :, 0], k_norm_w)
    return q, k, kv[:, :, 1]


def _attend_blocks(q, k, v):
    b, l, hq, d = q.shape
    grp = hq // ATTN_KV_HEADS
    nb = l // Q_BLOCK
    qb = jnp.moveaxis(q.reshape(b, nb, Q_BLOCK, ATTN_KV_HEADS, grp, d), 1, 0)
    scale = d ** -0.5

    def one_block(qi):
        s = jnp.einsum('bqhgd,bkhd->bhgqk', qi, k).astype(jnp.float32) * scale
        p = jax.nn.softmax(s, axis=-1).astype(v.dtype)
        return jnp.einsum('bhgqk,bkhd->bqhgd', p, v)

    o = lax.map(one_block, qb)
    return jnp.moveaxis(o, 0, 1).reshape(b, l, hq, d)


def _axial_gqa(lat, con, q_norm_w, k_norm_w, rope_cos, rope_sin, need_ctx):
    ql, kl, vl = _attn_prep(lat[0], lat[1], q_norm_w, k_norm_w)
    qc, kc, vc = _attn_prep(con[0], con[1], q_norm_w, k_norm_w)
    ql = _apply_axial_rope(ql, rope_cos, rope_sin)
    kl = _apply_axial_rope(kl, rope_cos, rope_sin)
    k_all = jnp.concatenate([kl, kc], axis=1)
    v_all = jnp.concatenate([vl, vc], axis=1)
    b, l = ql.shape[:2]
    out_l = _attend_blocks(ql, k_all, v_all).reshape(b, l, ATTN_WIDTH)
    out_c = _attend_blocks(qc, kc, vc).reshape(b, qc.shape[1], ATTN_WIDTH) if need_ctx else None
    return out_l, out_c


def _fourier_mix(u, w):
    b, l, _ = u.shape
    ug = u.astype(jnp.float32).reshape(b, l, FNET_GROUPS, FNET_GROUP_DIM)
    f = jnp.fft.fft2(ug, axes=(1, 3), norm="ortho").real
    return f.reshape(b, l, FNET_WIDTH).astype(u.dtype) @ w


def _token_mixers(pl, pc, conv_w, a_log, dt_bias, dn_norm_w, q_norm_w, k_norm_w, fnet_w,
                  rope_cos, rope_sin, need_ctx):
    lat = _split_in(pl)
    con = _split_in(pc)
    dl, dc = _deltanet(lat[0:4], con[0:4], conv_w, a_log, dt_bias, dn_norm_w, need_ctx)
    al, ac = _axial_gqa(lat[4:6], con[4:6], q_norm_w, k_norm_w, rope_cos, rope_sin, need_ctx)
    ml = jnp.concatenate([dl, al, _fourier_mix(lat[6], fnet_w)], axis=-1)
    mc = jnp.concatenate([dc, ac, _fourier_mix(con[6], fnet_w)], axis=-1) if need_ctx else None
    return ml, mc


def _swiglu(h, wg, wu, wd):
    return (jax.nn.silu(h @ wg) * (h @ wu)) @ wd


def _moe_swiglu(h, router, wg, wu, wd):
    logits = (h @ router).astype(jnp.float32)
    top_v, top_i = lax.top_k(logits, TOP_K)
    gates = jax.nn.softmax(top_v, axis=-1)
    combine = jnp.sum(jax.nn.one_hot(top_i, N_EXPERTS, dtype=jnp.float32) * gates[..., None], axis=-2)
    out = combine[..., 0:1].astype(h.dtype) * _swiglu(h, wg[0], wu[0], wd[0])
    for e in range(1, N_EXPERTS):
        out = out + combine[..., e:e + 1].astype(h.dtype) * _swiglu(h, wg[e], wu[e], wd[e])
    return out


def setup_inputs(seed: int = 0) -> dict:
    key = jax.random.key(seed)
    ks = jax.random.split(key, 26)
    f32 = jnp.float32
    nrm = lambda k, shape, s: jax.random.normal(k, shape, f32) * s
    d = D_MODEL
    dt = jnp.exp(jax.random.uniform(ks[7], (DEPTH, 2, DN_HEADS), f32, math.log(DT_MIN), math.log(DT_MAX)))
    return {
        "x": nrm(ks[0], (BATCH, SEQ, d), 1.0),
        "c": nrm(ks[1], (BATCH, d), 1.0),
        "ctx": nrm(ks[2], (BATCH, CTX_LEN, d), 1.0),
        "c_ctx": nrm(ks[3], (d,), 1.0),
        "w_mod": nrm(ks[4], (DEPTH, d, 6 * d), d ** -0.5),
        "b_mod": nrm(ks[5], (DEPTH, 6 * d), 0.02),
        "w_in": nrm(ks[6], (DEPTH, d, IN_COLS), d ** -0.5),
        "dn_conv": nrm(ks[8], (DEPTH, CONV_K, 3 * DN_WIDTH), CONV_K ** -0.5),
        "dn_a_log": jnp.log(jax.random.uniform(ks[9], (DEPTH, 2, DN_HEADS), f32, 1.0, 16.0)),
        "dn_dt_bias": dt + jnp.log(-jnp.expm1(-dt)),
        "dn_norm": 1.0 + nrm(ks[10], (DEPTH, HEAD_DIM), 0.02),
        "attn_q_norm": 1.0 + nrm(ks[11], (DEPTH, HEAD_DIM), 0.02),
        "attn_k_norm": 1.0 + nrm(ks[12], (DEPTH, HEAD_DIM), 0.02),
        "fnet_w": nrm(ks[13], (DEPTH, FNET_WIDTH, FNET_WIDTH), FNET_WIDTH ** -0.5),
        "w_out": nrm(ks[14], (DEPTH, MIX_WIDTH, d), MIX_WIDTH ** -0.5 * DEEPNORM_BETA),
        "ln1_g": 1.0 + nrm(ks[15], (DEPTH, d), 0.02),
        "ln1_b": nrm(ks[16], (DEPTH, d), 0.02),
        "ln2_g": 1.0 + nrm(ks[17], (DEPTH, d), 0.02),
        "ln2_b": nrm(ks[18], (DEPTH, d), 0.02),
        "ffn_w_gate": nrm(ks[19], (N_DENSE, d, D_FF), d ** -0.5),
        "ffn_w_up": nrm(ks[20], (N_DENSE, d, D_FF), d ** -0.5),
        "ffn_w_down": nrm(ks[21], (N_DENSE, D_FF, d), D_FF ** -0.5 * DEEPNORM_BETA),
        "router": nrm(ks[22], (N_MOE, d, N_EXPERTS), d ** -0.5),
        "moe_w_gate": nrm(ks[23], (N_MOE, N_EXPERTS, d, D_FF), d ** -0.5),
        "moe_w_up": nrm(ks[24], (N_MOE, N_EXPERTS, d, D_FF), d ** -0.5),
        "moe_w_down": nrm(ks[25], (N_MOE, N_EXPERTS, D_FF, d), D_FF ** -0.5 * DEEPNORM_BETA),
    }


def reference(x, c, ctx, c_ctx, w_mod, b_mod, w_in, dn_conv, dn_a_log, dn_dt_bias, dn_norm,
              attn_q_norm, attn_k_norm, fnet_w, w_out, ln1_g, ln1_b, ln2_g, ln2_b,
              ffn_w_gate, ffn_w_up, ffn_w_down, router, moe_w_gate, moe_w_up, moe_w_down):
    rows = x.shape[1] // GRID_W
    rope_cos, rope_sin = _axial_rope_tables(rows)
    silu_c = jax.nn.silu(c)
    silu_cc = jax.nn.silu(c_ctx)
    xl, xc = x, ctx
    for layer in range(DEPTH):
        need_ctx = layer < DEPTH - 1
        mod_l = jnp.split((silu_c @ w_mod[layer] + b_mod[layer])[:, None, :], 6, axis=-1)
        mod_c = jnp.split(silu_cc @ w_mod[layer] + b_mod[layer], 6, axis=-1)
        pl = _modulate(xl, mod_l[0], mod_l[1]) @ w_in[layer]
        pc = _modulate(xc, mod_c[0], mod_c[1]) @ w_in[layer]
        ml, mc = _token_mixers(pl, pc, dn_conv[layer], dn_a_log[layer], dn_dt_bias[layer], dn_norm[layer],
                               attn_q_norm[layer], attn_k_norm[layer], fnet_w[layer],
                               rope_cos, rope_sin, need_ctx)
        xl = _layer_norm(DEEPNORM_ALPHA * xl + mod_l[2] * (ml @ w_out[layer]), ln1_g[layer], ln1_b[layer])
        if need_ctx:
            xc = _layer_norm(DEEPNORM_ALPHA * xc + mod_c[2] * (mc @ w_out[layer]), ln1_g[layer], ln1_b[layer])
        i = layer // 2
        if layer % 2 == 0:
            channel_mix = lambda h: _swiglu(h, ffn_w_gate[i], ffn_w_up[i], ffn_w_down[i])
        else:
            channel_mix = lambda h: _moe_swiglu(h, router[i], moe_w_gate[i], moe_w_up[i], moe_w_down[i])
        yl = channel_mix(_modulate(xl, mod_l[3], mod_l[4]))
        xl = _layer_norm(DEEPNORM_ALPHA * xl + mod_l[5] * yl, ln2_g[layer], ln2_b[layer])
        if need_ctx:
            yc = channel_mix(_modulate(xc, mod_c[3], mod_c[4]))
            xc = _layer_norm(DEEPNORM_ALPHA * xc + mod_c[5] * yc, ln2_g[layer], ln2_b[layer])
    return xl
```

```python
import functools
import math

import jax
import jax.numpy as jnp
from jax import lax
from jax.experimental import pallas as pl
from jax.experimental.pallas import tpu as pltpu

F32 = jnp.float32
MXU_DT = jnp.bfloat16
HI = lax.Precision.HIGHEST

EPS = 1e-6
HEAD = 128
DN_HEADS = 8
AQ_HEADS = 4
AKV_HEADS = 2
FN_GROUPS = 4
CHUNK = 64
CONV_K = 5
GRID_W = 64
ROPE_THETA = 10000.0
N_EXPERTS = 8
DN_W = DN_HEADS * HEAD
AQ_W = AQ_HEADS * HEAD
AKV_W = AKV_HEADS * HEAD
FN_W = FN_GROUPS * HEAD
C_DNQ, C_DNK, C_DNV = 0, DN_W, 2 * DN_W
C_GATE = 3 * DN_W
C_AQ = 4 * DN_W
C_AK = C_AQ + AQ_W
C_AV = C_AK + AKV_W
C_FN = C_AV + AKV_W
P_COLS = C_FN + FN_W

TM = 512
RB = 256
TQ = 128
V7X_VMEM_LIMIT = 56 * 1024 * 1024


def _cp(sem, vmem=V7X_VMEM_LIMIT):
    return pltpu.CompilerParams(dimension_semantics=sem, vmem_limit_bytes=vmem)


def _silu(x):
    return x * jax.nn.sigmoid(x)


def _ln(x):
    mu = jnp.mean(x, axis=-1, keepdims=True)
    xc = x - mu
    var = jnp.mean(xc * xc, axis=-1, keepdims=True)
    return xc * lax.rsqrt(var + EPS)


def _dot(a, b):
    return jnp.dot(a.astype(MXU_DT), b.astype(MXU_DT), preferred_element_type=F32)


def _dot_nt(a, b):
    return lax.dot_general(a.astype(MXU_DT), b.astype(MXU_DT), (((1,), (1,)), ((), ())),
                           preferred_element_type=F32)


def _mod_kernel(c_ref, w_ref, b_ref, o_ref):
    o_ref[...] = _dot(_silu(c_ref[...]), w_ref[...]) + b_ref[...]


def _modulation(c8, w_mod, b_mod, layer):
    d = c8.shape[1]
    n = w_mod.shape[2]
    tn = d // 2
    return pl.pallas_call(
        _mod_kernel,
        grid=(n // tn,),
        in_specs=[pl.BlockSpec((8, d), lambda j: (0, 0)),
                  pl.BlockSpec((None, d, tn), lambda j: (layer, 0, j)),
                  pl.BlockSpec((None, 1, tn), lambda j: (layer, 0, j))],
        out_specs=pl.BlockSpec((8, tn), lambda j: (0, j)),
        out_shape=jax.ShapeDtypeStruct((8, n), F32),
        compiler_params=_cp(("parallel",)),
        name="modulation",
    )(c8, w_mod, b_mod)


def _ln_mod_matmul_kernel(x_ref, mod_ref, w_ref, o_ref, h_ref, *, shift_row):
    @pl.when(pl.program_id(1) == 0)
    def _():
        h = _ln(x_ref[...]) * (1.0 + mod_ref[shift_row + 1:shift_row + 2, :]) + mod_ref[shift_row:shift_row + 1, :]
        h_ref[...] = h.astype(MXU_DT)

    o_ref[...] = jnp.dot(h_ref[...], w_ref[...], preferred_element_type=F32)


def _ln_mod_matmul(x, mod, w, layer, *, seg_of_tile, shift_row, tn):
    r, d = x.shape
    n = w.shape[2]
    return pl.pallas_call(
        functools.partial(_ln_mod_matmul_kernel, shift_row=shift_row),
        grid=(r // TM, n // tn),
        in_specs=[pl.BlockSpec((TM, d), lambda i, j: (i, 0)),
                  pl.BlockSpec((None, 6, d), lambda i, j: (seg_of_tile(i), 0, 0)),
                  pl.BlockSpec((None, d, tn), lambda i, j: (layer, 0, j))],
        out_specs=pl.BlockSpec((TM, tn), lambda i, j: (i, j)),
        out_shape=jax.ShapeDtypeStruct((r, n), F32),
        scratch_shapes=[pltpu.VMEM((TM, d), MXU_DT)],
        compiler_params=_cp(("parallel", "arbitrary")),
        name="in_proj",
    )(x, mod, w)


def _dn_conv_kernel(prev_ref, x_ref, next_ref, w_ref, o_ref, buf_ref, *, blocks_per_seq, n_lat_blocks):
    i = pl.program_id(0)
    part = pl.program_id(1)
    li = i % blocks_per_seq
    is_lat = i < n_lat_blocks
    prev_ok = jnp.logical_and(is_lat, li > 0)
    next_ok = jnp.logical_and(is_lat, li < blocks_per_seq - 1)
    buf_ref[0:8, :] = jnp.where(prev_ok, prev_ref[...], 0.0)
    buf_ref[8:8 + RB, :] = x_ref[...]
    buf_ref[8 + RB:16 + RB, :] = jnp.where(next_ok, next_ref[...], 0.0)
    pad = CONV_K // 2
    y = buf_ref[8 - pad:8 - pad + RB, :] * w_ref[0:1, :]
    for tap in range(1, CONV_K):
        y = y + buf_ref[8 - pad + tap:8 - pad + tap + RB, :] * w_ref[tap:tap + 1, :]
    y = _silu(y)
    normed = part < 2
    for h in range(DN_HEADS):
        yh = y[:, h * HEAD:(h + 1) * HEAD]
        inv = lax.rsqrt(jnp.sum(yh * yh, axis=-1, keepdims=True) + EPS)
        o_ref[:, h * HEAD:(h + 1) * HEAD] = yh * jnp.where(normed, inv, 1.0)


def _dn_conv(p, conv_w, layer, *, blocks_per_seq, n_lat_blocks):
    r = p.shape[0]
    nblk = r // RB
    sub = RB // 8
    last8 = r // 8 - 1
    return pl.pallas_call(
        functools.partial(_dn_conv_kernel, blocks_per_seq=blocks_per_seq, n_lat_blocks=n_lat_blocks),
        grid=(nblk, 3),
        in_specs=[pl.BlockSpec((8, DN_W), lambda i, c: (jnp.maximum(i * sub - 1, 0), c)),
                  pl.BlockSpec((RB, DN_W), lambda i, c: (i, c)),
                  pl.BlockSpec((8, DN_W), lambda i, c: (jnp.minimum((i + 1) * sub, last8), c)),
                  pl.BlockSpec((None, 8, DN_W), lambda i, c: (layer, 0, c))],
        out_specs=pl.BlockSpec((RB, DN_W), lambda i, c: (i, c)),
        out_shape=jax.ShapeDtypeStruct((r, 3 * DN_W), F32),
        scratch_shapes=[pltpu.VMEM((RB + 16, DN_W), F32)],
        compiler_params=_cp(("parallel", "parallel")),
        name="dn_conv",
    )(p, p, p, conv_w)


def _col(x, lane, idx):
    return jnp.sum(jnp.where(lane == idx, x, 0.0), axis=-1, keepdims=True)


def _dn_prep_kernel(q_ref, k_ref, v_ref, ps_ref, par_ref,
                    wq_f, u_f, kg_f, qk_f, dl_f, wq_b, u_b, kg_b, qk_b, dl_b):
    h = pl.program_id(1)
    nchunk = RB // CHUNK
    q = q_ref[...] * (HEAD ** -0.5)
    k = k_ref[...]
    v = v_ref[...]
    ps = ps_ref[...]
    lane = lax.broadcasted_iota(jnp.int32, (RB, HEAD), 1)
    rowc = lax.broadcasted_iota(jnp.int32, (RB, HEAD), 0) % CHUNK
    beta_all = jax.nn.sigmoid(ps)
    z = ps + par_ref[0:1, :]
    dt = jnp.maximum(z, 0.0) + jnp.log(1.0 + jnp.exp(-jnp.abs(z)))
    g_all = -jnp.exp(par_ref[1:2, :]) * dt
    gf = g_all
    gb = g_all
    s = 1
    while s < CHUNK:
        gf = gf + jnp.where(rowc >= s, pltpu.roll(gf, s, 0), 0.0)
        gb = gb + jnp.where(rowc < CHUNK - s, pltpu.roll(gb, RB - s, 0), 0.0)
        s *= 2
    tot_all = gf + gb - g_all

    r2 = lax.broadcasted_iota(jnp.int32, (RB, RB), 0)
    c2 = lax.broadcasted_iota(jnp.int32, (RB, RB), 1)
    eye = (r2 == c2).astype(F32)
    kk = _dot_nt(k, k)
    qk = _dot_nt(q, k)
    kgt_dummy = None
    outs = ((wq_f, u_f, kg_f, qk_f, dl_f), (wq_b, u_b, kg_b, qk_b, dl_b))
    for d in range(2):
        wq_ref, u_ref, kg_ref, qk_ref, dl_ref = outs[d]
        bcol = _col(beta_all, lane, d * DN_HEADS + h)
        gcol = _col(gf if d == 0 else gb, lane, 2 * DN_HEADS + d * DN_HEADS + h)
        tcol = _col(tot_all, lane, 2 * DN_HEADS + d * DN_HEADS + h)
        gi = jnp.broadcast_to(gcol, (RB, RB))
        gj = gi.T
        same = (r2 // CHUNK) == (c2 // CHUNK)
        causal = (r2 >= c2) if d == 0 else (r2 <= c2)
        incl = jnp.logical_and(same, causal)
        strict = jnp.logical_and(incl, r2 != c2)
        dm = jnp.where(incl, jnp.exp(jnp.minimum(gi - gj, 0.0)), 0.0)
        lm = jnp.where(strict, bcol * kk * dm, 0.0)
        t = eye - jnp.where((r2 // 2) == (c2 // 2), lm, 0.0)
        bs = 2
        while bs < CHUNK:
            off = jnp.where(jnp.logical_and((r2 // (2 * bs)) == (c2 // (2 * bs)), (r2 // bs) != (c2 // bs)), lm, 0.0)
            t = t - _dot(_dot(t, off), t)
            bs *= 2
        eg = jnp.exp(gcol)
        kb = k * bcol
        sol_u = _dot(t, v * bcol)
        sol_w = _dot(t, kb * eg)
        qkm = jnp.where(incl, qk * dm, 0.0)
        q_g = q * eg
        k_g = k * jnp.exp(tcol - gcol)
        k_gt = k_g.T
        dl = jnp.exp(tcol)
        for j in range(nchunk):
            slot = j if d == 0 else nchunk - 1 - j
            rows = slice(j * CHUNK, (j + 1) * CHUNK)
            wq_ref[0, slot, 0:CHUNK, :] = sol_w[rows, :]
            wq_ref[0, slot, CHUNK:2 * CHUNK, :] = q_g[rows, :]
            u_ref[0, slot, :, :] = sol_u[rows, :]
            kg_ref[0, slot, :, :] = k_gt[:, rows]
            qk_ref[0, slot, :, :] = qkm[rows, rows]
            dl_ref[0, slot, :, :] = jnp.broadcast_to(dl[j * CHUNK:j * CHUNK + 1, :], (8, HEAD))


def _dn_prep(qkv, ps, par, *, batch, blocks_per_seq):
    r = qkv.shape[0]
    nblk = r // RB
    n_lat = batch * blocks_per_seq
    nchunk = RB // CHUNK
    npos = (blocks_per_seq + 1) * nchunk
    chains = batch * DN_HEADS

    def bidx(i):
        return jnp.where(i < n_lat, i // blocks_per_seq, i - n_lat)

    def pos_f(i):
        return jnp.where(i < n_lat, 1 + i % blocks_per_seq, 0)

    def pos_b(i):
        return jnp.where(i < n_lat, blocks_per_seq - i % blocks_per_seq, 0)

    def ospec(shape, pos):
        return pl.BlockSpec((1, nchunk) + shape, lambda i, h: (bidx(i) * DN_HEADS + h, pos(i), 0, 0))

    shapes = ((2 * CHUNK, HEAD), (CHUNK, HEAD), (HEAD, CHUNK), (CHUNK, CHUNK), (8, HEAD))
    out_specs = [ospec(s, pos_f) for s in shapes] + [ospec(s, pos_b) for s in shapes]
    out_shape = [jax.ShapeDtypeStruct((chains, npos) + s, F32) for s in shapes] * 2
    return pl.pallas_call(
        _dn_prep_kernel,
        grid=(nblk, DN_HEADS),
        in_specs=[pl.BlockSpec((RB, HEAD), lambda i, h: (i, h)),
                  pl.BlockSpec((RB, HEAD), lambda i, h: (i, DN_HEADS + h)),
                  pl.BlockSpec((RB, HEAD), lambda i, h: (i, 2 * DN_HEADS + h)),
                  pl.BlockSpec((RB, HEAD), lambda i, h: (i, 0)),
                  pl.BlockSpec((8, HEAD), lambda i, h: (0, 0))],
        out_specs=out_specs,
        out_shape=out_shape,
        compiler_params=_cp(("parallel", "parallel")),
        name="dn_prep",
    )(qkv, qkv, qkv, ps, par)


def _dn_scan_kernel(wq_f, u_f, kg_f, qk_f, dl_f, wq_b, u_b, kg_b, qk_b, dl_b, o_f, o_b, s_ref):
    @pl.when(pl.program_id(1) == 0)
    def _():
        s_ref[...] = jnp.zeros_like(s_ref)

    ins = ((wq_f, u_f, kg_f, qk_f, dl_f, o_f), (wq_b, u_b, kg_b, qk_b, dl_b, o_b))
    for d in range(2):
        wq_ref, u_ref, kg_ref, qk_ref, dl_ref, o_ref = ins[d]
        for c in range(DN_HEADS):
            sc = d * DN_HEADS + c
            s = s_ref[sc]
            r = _dot(wq_ref[c, 0], s)
            v_new = u_ref[c, 0] - r[0:CHUNK, :]
            o_ref[c, 0] = r[CHUNK:2 * CHUNK, :] + _dot(qk_ref[c, 0], v_new)
            s_ref[sc] = s * dl_ref[c, 0, 0:1, :] + _dot(kg_ref[c, 0], v_new)


def _dn_scan(prep, *, batch):
    chains, npos = prep[0].shape[:2]
    shapes = ((2 * CHUNK, HEAD), (CHUNK, HEAD), (HEAD, CHUNK), (CHUNK, CHUNK), (8, HEAD))
    spec = lambda s: pl.BlockSpec((DN_HEADS, 1) + s, lambda b, n: (b, n, 0, 0))
    in_specs = [spec(s) for s in shapes] * 2
    o_sds = jax.ShapeDtypeStruct((chains, npos, CHUNK, HEAD), F32)
    return pl.pallas_call(
        _dn_scan_kernel,
        grid=(batch, npos),
        in_specs=in_specs,
        out_specs=[spec((CHUNK, HEAD)), spec((CHUNK, HEAD))],
        out_shape=[o_sds, o_sds],
        scratch_shapes=[pltpu.VMEM((2 * DN_HEADS, HEAD, HEAD), F32)],
        compiler_params=_cp(("parallel", "arbitrary")),
        name="dn_scan",
    )(*prep)


def _dn_out_kernel(of_ref, ob_ref, gate_ref, nw_ref, o_ref):
    nchunk = RB // CHUNK
    nw = nw_ref[0:1, :]
    for h in range(DN_HEADS):
        for j in range(nchunk):
            o = of_ref[h, j] + ob_ref[h, nchunk - 1 - j]
            y = o * lax.rsqrt(jnp.mean(o * o, axis=-1, keepdims=True) + EPS) * nw
            gt = gate_ref[j * CHUNK:(j + 1) * CHUNK, h * HEAD:(h + 1) * HEAD]
            o_ref[j * CHUNK:(j + 1) * CHUNK, h * HEAD:(h + 1) * HEAD] = (y * _silu(gt)).astype(o_ref.dtype)


def _dn_out(o_f, o_b, p, norm_w, *, batch, blocks_per_seq, nblk):
    n_lat = batch * blocks_per_seq
    nchunk = RB // CHUNK
    bidx = lambda i: jnp.where(i < n_lat, i // blocks_per_seq, i - n_lat)
    pos_f = lambda i: jnp.where(i < n_lat, 1 + i % blocks_per_seq, 0)
    pos_b = lambda i: jnp.where(i < n_lat, blocks_per_seq - i % blocks_per_seq, 0)
    return pl.pallas_call(
        _dn_out_kernel,
        grid=(nblk,),
        in_specs=[pl.BlockSpec((DN_HEADS, nchunk, CHUNK, HEAD), lambda i: (bidx(i), pos_f(i), 0, 0)),
                  pl.BlockSpec((DN_HEADS, nchunk, CHUNK, HEAD), lambda i: (bidx(i), pos_b(i), 0, 0)),
                  pl.BlockSpec((RB, DN_W), lambda i: (i, C_GATE // DN_W)),
                  pl.BlockSpec((8, HEAD), lambda i: (0, 0))],
        out_specs=pl.BlockSpec((RB, DN_W), lambda i: (i, 0)),
        out_shape=jax.ShapeDtypeStruct((nblk * RB, DN_W), MXU_DT),
        compiler_params=_cp(("parallel",)),
        name="dn_out",
    )(o_f, o_b, p, norm_w)


def _attn_prep_kernel(q_ref, k_ref, v_ref, cos_ref, sin_ref, qw_ref, kw_ref, qo_ref, ko_ref, vo_ref, *, n_lat_blocks):
    is_lat = pl.program_id(0) < n_lat_blocks
    cos = jnp.where(is_lat, cos_ref[...], 1.0)
    sin = jnp.where(is_lat, sin_ref[...], 0.0)
    lane = lax.broadcasted_iota(jnp.int32, (RB, HEAD), 1)
    first_half = (lane % (HEAD // 2)) < (HEAD // 4)

    def norm_rope(x, w):
        y = x * lax.rsqrt(jnp.mean(x * x, axis=-1, keepdims=True) + EPS) * w
        swapped = jnp.where(first_half, pltpu.roll(y, HEAD - HEAD // 4, 1), pltpu.roll(y, HEAD // 4, 1))
        return y * cos + swapped * sin

    for h in range(AQ_HEADS):
        qh = norm_rope(q_ref[:, h * HEAD:(h + 1) * HEAD], qw_ref[0:1, :]) * (HEAD ** -0.5)
        qo_ref[0, h] = qh.astype(qo_ref.dtype)
    for h in range(AKV_HEADS):
        ko_ref[0, h] = norm_rope(k_ref[:, h * HEAD:(h + 1) * HEAD], kw_ref[0:1, :]).astype(ko_ref.dtype)
        vo_ref[0, h] = v_ref[:, h * HEAD:(h + 1) * HEAD].astype(vo_ref.dtype)


def _attn_prep(p, cos, sin, qw, kw, *, batch, blocks_per_seq):
    r = p.shape[0]
    nblk = r // RB
    n_lat = batch * blocks_per_seq
    ltot = (blocks_per_seq + 1) * RB
    bidx = lambda i: jnp.where(i < n_lat, i // blocks_per_seq, i - n_lat)
    lblk = lambda i: jnp.where(i < n_lat, i % blocks_per_seq, blocks_per_seq)
    tblk = lambda i: jnp.where(i < n_lat, i % blocks_per_seq, 0)
    return pl.pallas_call(
        functools.partial(_attn_prep_kernel, n_lat_blocks=n_lat),
        grid=(nblk,),
        in_specs=[pl.BlockSpec((RB, AQ_W), lambda i: (i, C_AQ // AQ_W)),
                  pl.BlockSpec((RB, AKV_W), lambda i: (i, C_AK // AKV_W)),
                  pl.BlockSpec((RB, AKV_W), lambda i: (i, C_AV // AKV_W)),
                  pl.BlockSpec((RB, HEAD), lambda i: (tblk(i), 0)),
                  pl.BlockSpec((RB, HEAD), lambda i: (tblk(i), 0)),
                  pl.BlockSpec((8, HEAD), lambda i: (0, 0)),
                  pl.BlockSpec((8, HEAD), lambda i: (0, 0))],
        out_specs=[pl.BlockSpec((1, AQ_HEADS, RB, HEAD), lambda i: (bidx(i), 0, lblk(i), 0)),
                   pl.BlockSpec((1, AKV_HEADS, RB, HEAD), lambda i: (bidx(i), 0, lblk(i), 0)),
                   pl.BlockSpec((1, AKV_HEADS, RB, HEAD), lambda i: (bidx(i), 0, lblk(i), 0))],
        out_shape=[jax.ShapeDtypeStruct((batch, AQ_HEADS, ltot, HEAD), MXU_DT),
                   jax.ShapeDtypeStruct((batch, AKV_HEADS, ltot, HEAD), MXU_DT),
                   jax.ShapeDtypeStruct((batch, AKV_HEADS, ltot, HEAD), MXU_DT)],
        compiler_params=_cp(("parallel",)),
        name="attn_prep",
    )(p, p, p, cos, sin, qw, kw)


def _attn_kernel(q_ref, k_ref, v_ref, o_ref, *, n_lat_qblocks, lat_len):
    grp = AQ_HEADS // AKV_HEADS

    def attend(k, v):
        q = q_ref[0].reshape(grp * TQ, HEAD)
        s = lax.dot_general(q, k, (((1,), (1,)), ((), ())), preferred_element_type=F32)
        m = jnp.max(s, axis=-1, keepdims=True)
        e = jnp.exp(s - m)
        den = jnp.sum(e, axis=-1, keepdims=True)
        o = jnp.dot(e.astype(MXU_DT), v, preferred_element_type=F32) / den
        for g in range(grp):
            o_ref[:, g * HEAD:(g + 1) * HEAD] = o[g * TQ:(g + 1) * TQ, :].astype(o_ref.dtype)

    qi = pl.program_id(2)

    @pl.when(qi < n_lat_qblocks)
    def _():
        attend(k_ref[0, 0], v_ref[0, 0])

    @pl.when(qi >= n_lat_qblocks)
    def _():
        attend(k_ref[0, 0, lat_len:, :], v_ref[0, 0, lat_len:, :])


def _attention(qh, kh, vh, *, lat_len, with_ctx, n_rows):
    batch, _, ltot, _ = qh.shape
    grp = AQ_HEADS // AKV_HEADS
    n_lat_q = lat_len // TQ
    n_q = ltot // TQ if with_ctx else n_lat_q
    n_ctx_q = (ltot - lat_len) // TQ

    def orow(b, qi):
        return jnp.where(qi < n_lat_q, b * n_lat_q + qi, batch * n_lat_q + b * n_ctx_q + (qi - n_lat_q))

    return pl.pallas_call(
        functools.partial(_attn_kernel, n_lat_qblocks=n_lat_q, lat_len=lat_len),
        grid=(batch, AKV_HEADS, n_q),
        in_specs=[pl.BlockSpec((1, grp, TQ, HEAD), lambda b, g, qi: (b, g, qi, 0)),
                  pl.BlockSpec((1, 1, ltot, HEAD), lambda b, g, qi: (b, g, 0, 0)),
                  pl.BlockSpec((1, 1, ltot, HEAD), lambda b, g, qi: (b, g, 0, 0))],
        out_specs=pl.BlockSpec((TQ, grp * HEAD), lambda b, g, qi: (orow(b, qi), g)),
        out_shape=jax.ShapeDtypeStruct((n_rows, AQ_W), MXU_DT),
        compiler_params=_cp(("parallel", "parallel", "arbitrary")),
        name="attention",
    )(qh, kh, vh)


def _plain_matmul_kernel(x_ref, w_ref, o_ref):
    o_ref[...] = _dot(x_ref[...], w_ref[...]).astype(o_ref.dtype)


def _fnet_channel_dft(p, cs):
    r = p.shape[0]
    return pl.pallas_call(
        _plain_matmul_kernel,
        grid=(r // TM,),
        in_specs=[pl.BlockSpec((TM, FN_W), lambda i: (i, C_FN // FN_W)),
                  pl.BlockSpec((FN_W, 2 * FN_W), lambda i: (0, 0))],
        out_specs=pl.BlockSpec((TM, 2 * FN_W), lambda i: (i, 0)),
        out_shape=jax.ShapeDtypeStruct((r, 2 * FN_W), MXU_DT),
        compiler_params=_cp(("parallel",)),
        name="fnet_channel_dft",
    )(p, cs)


def _fnet_pos_kernel(*refs, batch):
    cos_ref, sin_ref = refs[0], refs[1]
    y_refs = refs[2:2 + batch]
    w_ref = refs[2 + batch]
    o_ref = refs[3 + batch]
    acc_ref = refs[4 + batch]
    kk = pl.program_id(1)

    @pl.when(kk == 0)
    def _():
        acc_ref[...] = jnp.zeros_like(acc_ref)

    c = cos_ref[...]
    s = sin_ref[...]
    for b in range(batch):
        y = y_refs[b][...]
        acc_ref[b] += (jnp.dot(c, y[:, 0:FN_W], preferred_element_type=F32)
                       - jnp.dot(s, y[:, FN_W:2 * FN_W], preferred_element_type=F32))

    @pl.when(kk == pl.num_programs(1) - 1)
    def _():
        for b in range(batch):
            o_ref[b] = _dot(acc_ref[b], w_ref[...]).astype(o_ref.dtype)


def _fnet_pos_dft(y, cos_t, sin_t, fnet_w, layer, *, batch, seq_len, row_off):
    t = min(1024, seq_len)
    nt = seq_len // t
    y_specs = [pl.BlockSpec((t, 2 * FN_W), functools.partial(lambda i, k, b: (row_off // t + b * nt + k, 0), b=b))
               for b in range(batch)]
    return pl.pallas_call(
        functools.partial(_fnet_pos_kernel, batch=batch),
        grid=(nt, nt),
        in_specs=[pl.BlockSpec((t, t), lambda i, k: (i, k)),
                  pl.BlockSpec((t, t), lambda i, k: (i, k))] + y_specs +
                 [pl.BlockSpec((None, FN_W, FN_W), lambda i, k: (layer, 0, 0))],
        out_specs=pl.BlockSpec((batch, t, FN_W), lambda i, k: (0, i, 0)),
        out_shape=jax.ShapeDtypeStruct((batch, seq_len, FN_W), MXU_DT),
        scratch_shapes=[pltpu.VMEM((batch, t, FN_W), F32)],
        compiler_params=_cp(("parallel", "arbitrary")),
        name="fnet_pos_dft",
    )(cos_t, sin_t, *([y] * batch), fnet_w)


def _dft_tables(n):
    idx = jnp.arange(n, dtype=jnp.int32)
    ang = ((idx[:, None] * idx[None, :]) % n).astype(F32) * (2.0 * math.pi / n)
    scale = n ** -0.5
    return jnp.cos(ang) * scale, jnp.sin(ang) * scale


def _out_proj_kernel(x_ref, dn_ref, at_ref, fn_ref, w_ref, mod_ref, g_ref, b_ref, o_ref, *, alpha):
    acc = jnp.dot(dn_ref[...], w_ref[0:DN_W, :], preferred_element_type=F32)
    acc += jnp.dot(at_ref[...], w_ref[DN_W:DN_W + AQ_W, :], preferred_element_type=F32)
    acc += jnp.dot(fn_ref[...], w_ref[DN_W + AQ_W:, :], preferred_element_type=F32)
    y = alpha * x_ref[...] + mod_ref[2:3, :] * acc
    o_ref[...] = _ln(y) * g_ref[0:1, :] + b_ref[0:1, :]


def _out_proj(x, a_dn, a_at, a_fn, w_out, mod, g, b, layer, *, seg_of_tile, n_rows, alpha):
    d = x.shape[1]
    tm = RB
    seg = lambda i: seg_of_tile(i // (TM // tm))
    return pl.pallas_call(
        functools.partial(_out_proj_kernel, alpha=alpha),
        grid=(n_rows // tm,),
        in_specs=[pl.BlockSpec((tm, d), lambda i: (i, 0)),
                  pl.BlockSpec((tm, DN_W), lambda i: (i, 0)),
                  pl.BlockSpec((tm, AQ_W), lambda i: (i, 0)),
                  pl.BlockSpec((tm, FN_W), lambda i: (i, 0)),
                  pl.BlockSpec((None, DN_W + AQ_W + FN_W, d), lambda i: (layer, 0, 0)),
                  pl.BlockSpec((None, 6, d), lambda i: (seg(i), 0, 0)),
                  pl.BlockSpec((None, 1, d), lambda i: (layer, 0, 0)),
                  pl.BlockSpec((None, 1, d), lambda i: (layer, 0, 0))],
        out_specs=pl.BlockSpec((tm, d), lambda i: (i, 0)),
        out_shape=jax.ShapeDtypeStruct((n_rows, d), F32),
        compiler_params=_cp(("parallel",)),
        name="out_proj",
    )(x, a_dn, a_at, a_fn, w_out, mod, g, b)


def _ffn_kernel(x_ref, mod_ref, wg_ref, wu_ref, wd_ref, g_ref, b_ref, o_ref, h_ref, acc_ref, *, alpha):
    f = pl.program_id(1)

    @pl.when(f == 0)
    def _():
        h = _ln(x_ref[...]) * (1.0 + mod_ref[4:5, :]) + mod_ref[3:4, :]
        h_ref[...] = h.astype(MXU_DT)
        acc_ref[...] = jnp.zeros_like(acc_ref)

    h = h_ref[...]
    gate = jnp.dot(h, wg_ref[...], preferred_element_type=F32)
    up = jnp.dot(h, wu_ref[...], preferred_element_type=F32)
    acc_ref[...] += jnp.dot((_silu(gate) * up).astype(MXU_DT), wd_ref[...], preferred_element_type=F32)

    @pl.when(f == pl.num_programs(1) - 1)
    def _():
        y = alpha * x_ref[...] + mod_ref[5:6, :] * acc_ref[...]
        o_ref[...] = _ln(y) * g_ref[0:1, :] + b_ref[0:1, :]


def _ffn(x, mod, wg, wu, wd, g, b, layer, li, *, seg_of_tile, n_rows, alpha, tf):
    d = x.shape[1]
    ff = wg.shape[2]
    return pl.pallas_call(
        functools.partial(_ffn_kernel, alpha=alpha),
        grid=(n_rows // TM, ff // tf),
        in_specs=[pl.BlockSpec((TM, d), lambda i, f: (i, 0)),
                  pl.BlockSpec((None, 6, d), lambda i, f: (seg_of_tile(i), 0, 0)),
                  pl.BlockSpec((None, d, tf), lambda i, f: (li, 0, f)),
                  pl.BlockSpec((None, d, tf), lambda i, f: (li, 0, f)),
                  pl.BlockSpec((None, tf, d), lambda i, f: (li, f, 0)),
                  pl.BlockSpec((None, 1, d), lambda i, f: (layer, 0, 0)),
                  pl.BlockSpec((None, 1, d), lambda i, f: (layer, 0, 0))],
        out_specs=pl.BlockSpec((TM, d), lambda i, f: (i, 0)),
        out_shape=jax.ShapeDtypeStruct((n_rows, d), F32),
        scratch_shapes=[pltpu.VMEM((TM, d), MXU_DT), pltpu.VMEM((TM, d), F32)],
        compiler_params=_cp(("parallel", "arbitrary")),
        name="ffn",
    )(x, mod, wg, wu, wd, g, b)


def _router_kernel(x_ref, mod_ref, rw_ref, h_ref, idx_ref, gate_ref):
    h = _ln(x_ref[...]) * (1.0 + mod_ref[4:5, :]) + mod_ref[3:4, :]
    h_ref[...] = h.astype(h_ref.dtype)
    logits = jnp.dot(h, rw_ref[...], precision=HI, preferred_element_type=F32)
    lane = lax.broadcasted_iota(jnp.int32, logits.shape, 1)
    neg = jnp.float32(-jnp.inf)
    l1 = jnp.where(lane < N_EXPERTS, logits, neg)
    m1 = jnp.max(l1, axis=-1, keepdims=True)
    i1 = jnp.min(jnp.where(l1 == m1, lane, HEAD), axis=-1, keepdims=True)
    l2 = jnp.where(lane == i1, neg, l1)
    m2 = jnp.max(l2, axis=-1, keepdims=True)
    i2 = jnp.min(jnp.where(l2 == m2, lane, HEAD), axis=-1, keepdims=True)
    e = jnp.exp(m2 - m1)
    g1 = 1.0 / (1.0 + e)
    g2 = e / (1.0 + e)
    idx_ref[...] = jnp.where(lane == 0, i1, jnp.where(lane == 1, i2, 0))
    gate_ref[...] = jnp.where(lane == 0, g1, jnp.where(lane == 1, g2, 0.0))


def _router(x, mod, rw, *, seg_of_tile, n_rows):
    d = x.shape[1]
    return pl.pallas_call(
        _router_kernel,
        grid=(n_rows // TM,),
        in_specs=[pl.BlockSpec((TM, d), lambda i: (i, 0)),
                  pl.BlockSpec((None, 6, d), lambda i: (seg_of_tile(i), 0, 0)),
                  pl.BlockSpec((d, HEAD), lambda i: (0, 0))],
        out_specs=[pl.BlockSpec((TM, d), lambda i: (i, 0)),
                   pl.BlockSpec((TM, HEAD), lambda i: (i, 0)),
                   pl.BlockSpec((TM, HEAD), lambda i: (i, 0))],
        out_shape=[jax.ShapeDtypeStruct((n_rows, d), MXU_DT),
                   jax.ShapeDtypeStruct((n_rows, HEAD), jnp.int32),
                   jax.ShapeDtypeStruct((n_rows, HEAD), F32)],
        compiler_params=_cp(("parallel",)),
        name="router",
    )(x, mod, rw)


def _moe_kernel(te_ref, nv_ref, h_ref, wg_ref, wu_ref, wd_ref, o_ref, acc_ref):
    t = pl.program_id(0)
    f = pl.program_id(1)
    last = pl.num_programs(1) - 1
    valid = t < nv_ref[0]

    @pl.when(jnp.logical_and(valid, f == 0))
    def _():
        acc_ref[...] = jnp.zeros_like(acc_ref)

    @pl.when(valid)
    def _():
        h = h_ref[...]
        gate = jnp.dot(h, wg_ref[...], preferred_element_type=F32)
        up = jnp.dot(h, wu_ref[...], preferred_element_type=F32)
        acc_ref[...] += jnp.dot((_silu(gate) * up).astype(MXU_DT), wd_ref[...], preferred_element_type=F32)

    @pl.when(jnp.logical_and(valid, f == last))
    def _():
        o_ref[...] = acc_ref[...]

    @pl.when(jnp.logical_and(jnp.logical_not(valid), f == last))
    def _():
        o_ref[...] = jnp.zeros_like(o_ref)


def _moe_experts(tile_expert, n_valid, h_sorted, wg, wu, wd, mi, *, tf):
    mp, d = h_sorted.shape
    ff = wg.shape[3]
    nf = ff // tf

    def fidx(t, f, nv):
        return jnp.where(t < nv[0], f, nf - 1)

    grid_spec = pltpu.PrefetchScalarGridSpec(
        num_scalar_prefetch=2,
        grid=(mp // TM, nf),
        in_specs=[pl.BlockSpec((TM, d), lambda t, f, te, nv: (t, 0)),
                  pl.BlockSpec((None, None, d, tf), lambda t, f, te, nv: (mi, te[t], 0, fidx(t, f, nv))),
                  pl.BlockSpec((None, None, d, tf), lambda t, f, te, nv: (mi, te[t], 0, fidx(t, f, nv))),
                  pl.BlockSpec((None, None, tf, d), lambda t, f, te, nv: (mi, te[t], fidx(t, f, nv), 0))],
        out_specs=pl.BlockSpec((TM, d), lambda t, f, te, nv: (t, 0)),
        scratch_shapes=[pltpu.VMEM((TM, d), F32)],
    )
    return pl.pallas_call(
        _moe_kernel,
        grid_spec=grid_spec,
        out_shape=jax.ShapeDtypeStruct((mp, d), F32),
        compiler_params=_cp(("arbitrary", "arbitrary")),
        name="moe_experts",
    )(tile_expert, n_valid, h_sorted, wg, wu, wd)


def _moe_combine_kernel(x_ref, y1_ref, y2_ref, gate_ref, mod_ref, g_ref, b_ref, o_ref, *, alpha):
    gates = gate_ref[...]
    y = gates[:, 0:1] * y1_ref[...] + gates[:, 1:2] * y2_ref[...]
    z = alpha * x_ref[...] + mod_ref[5:6, :] * y
    o_ref[...] = _ln(z) * g_ref[0:1, :] + b_ref[0:1, :]


def _moe_combine(x, y1, y2, gates, mod, g, b, layer, *, seg_of_tile, n_rows, alpha):
    d = x.shape[1]
    tm = RB
    seg = lambda i: seg_of_tile(i // (TM // tm))
    return pl.pallas_call(
        functools.partial(_moe_combine_kernel, alpha=alpha),
        grid=(n_rows // tm,),
        in_specs=[pl.BlockSpec((tm, d), lambda i: (i, 0)),
                  pl.BlockSpec((tm, d), lambda i: (i, 0)),
                  pl.BlockSpec((tm, d), lambda i: (i, 0)),
                  pl.BlockSpec((tm, HEAD), lambda i: (i, 0)),
                  pl.BlockSpec((None, 6, d), lambda i: (seg(i), 0, 0)),
                  pl.BlockSpec((None, 1, d), lambda i: (layer, 0, 0)),
                  pl.BlockSpec((None, 1, d), lambda i: (layer, 0, 0))],
        out_specs=pl.BlockSpec((tm, d), lambda i: (i, 0)),
        out_shape=jax.ShapeDtypeStruct((n_rows, d), F32),
        compiler_params=_cp(("parallel",)),
        name="moe_combine",
    )(x, y1, y2, gates, mod, g, b)


def _moe_layer(x, mod, rw, wg, wu, wd, g, b, layer, mi, *, seg_of_tile, n_rows, alpha, tf):
    h, idx, gates = _router(x, mod, rw, seg_of_tile=seg_of_tile, n_rows=n_rows)
    na = 2 * n_rows
    mp = na + N_EXPERTS * TM
    e_flat = idx[:, 0:2].reshape(na)
    onehot = (e_flat[:, None] == jnp.arange(N_EXPERTS, dtype=jnp.int32)[None, :]).astype(jnp.int32)
    csum = jnp.cumsum(onehot, axis=0)
    rank = jnp.sum((csum - onehot) * onehot, axis=1)
    counts = csum[-1]
    padded = ((counts + TM - 1) // TM) * TM
    ends = jnp.cumsum(padded)
    dest = (ends - padded)[e_flat] + rank
    src_token = jnp.zeros((mp,), jnp.int32).at[dest].set(jnp.arange(na, dtype=jnp.int32) // 2)
    tile_start = jnp.arange(mp // TM, dtype=jnp.int32) * TM
    tile_expert = jnp.minimum(jnp.searchsorted(ends, tile_start, side="right"), N_EXPERTS - 1).astype(jnp.int32)
    n_valid = (ends[-1] // TM).astype(jnp.int32).reshape(1)
    h_sorted = jnp.take(h, src_token, axis=0)
    y_sorted = _moe_experts(tile_expert, n_valid, h_sorted, wg, wu, wd, mi, tf=tf)
    dest2 = dest.reshape(n_rows, 2)
    y1 = jnp.take(y_sorted, dest2[:, 0], axis=0)
    y2 = jnp.take(y_sorted, dest2[:, 1], axis=0)
    return _moe_combine(x, y1, y2, gates, mod, g, b, layer, seg_of_tile=seg_of_tile, n_rows=n_rows, alpha=alpha)


def _rope_tables(seq_len):
    axis_dim = HEAD // 2
    inv = ROPE_THETA ** (-jnp.arange(0, axis_dim, 2, dtype=F32) / axis_dim)
    t = jnp.arange(seq_len, dtype=jnp.int32)
    row = (t // GRID_W).astype(F32)[:, None] * inv
    col = (t % GRID_W).astype(F32)[:, None] * inv
    cos = jnp.concatenate([jnp.cos(row), jnp.cos(row), jnp.cos(col), jnp.cos(col)], axis=-1)
    sin = jnp.concatenate([-jnp.sin(row), jnp.sin(row), -jnp.sin(col), jnp.sin(col)], axis=-1)
    return cos, sin


def _pad_rows8(v):
    return jnp.zeros((8, v.shape[-1]), F32).at[0].set(v.astype(F32))


def kernel(x, c, ctx, c_ctx, w_mod, b_mod, w_in, dn_conv, dn_a_log, dn_dt_bias, dn_norm, attn_q_norm, attn_k_norm,
           fnet_w, w_out, ln1_g, ln1_b, ln2_g, ln2_b, ffn_w_gate, ffn_w_up, ffn_w_down, router, moe_w_gate,
           moe_w_up, moe_w_down):
    batch, seq, d = x.shape
    ctx_len = ctx.shape[1]
    depth = w_mod.shape[0]
    ff = ffn_w_gate.shape[2]
    assert ctx_len == RB and seq % TM == 0 and (batch * ctx_len) % TM == 0 and TM % RB == 0
    alpha = (2 * depth) ** 0.25
    n_lat = batch * seq
    n_all = n_lat + batch * ctx_len
    tiles_per_seq = seq // TM
    blocks_per_seq = seq // RB
    seg_of_tile = lambda i: jnp.minimum(i // tiles_per_seq, batch)
    tf = 512 if ff % 512 == 0 else ff

    w_in_big = jnp.concatenate([w_in[:, :, :4 * DN_W], w_in[:, :, 4 * DN_W + 4 * DN_HEADS:]], axis=-1).astype(MXU_DT)
    w_in_small = jnp.pad(w_in[:, :, 4 * DN_W:4 * DN_W + 4 * DN_HEADS],
                         ((0, 0), (0, 0), (0, HEAD - 4 * DN_HEADS))).astype(MXU_DT)
    conv_w = jnp.pad(dn_conv, ((0, 0), (0, 8 - CONV_K), (0, 0)))
    w_out_c = w_out.astype(MXU_DT)
    fnet_w_c = fnet_w.astype(MXU_DT)
    ffn_wg, ffn_wu, ffn_wd = ffn_w_gate.astype(MXU_DT), ffn_w_up.astype(MXU_DT), ffn_w_down.astype(MXU_DT)
    moe_wg, moe_wu, moe_wd = moe_w_gate.astype(MXU_DT), moe_w_up.astype(MXU_DT), moe_w_down.astype(MXU_DT)
    router_p = jnp.pad(router, ((0, 0), (0, 0), (0, HEAD - N_EXPERTS)))
    b_mod3 = b_mod[:, None, :]
    ln1_g3, ln1_b3, ln2_g3, ln2_b3 = ln1_g[:, None, :], ln1_b[:, None, :], ln2_g[:, None, :], ln2_b[:, None, :]

    rope_cos, rope_sin = _rope_tables(seq)
    lat_cos, lat_sin = (t.astype(MXU_DT) for t in _dft_tables(seq))
    ctx_cos, ctx_sin = (t.astype(MXU_DT) for t in _dft_tables(ctx_len))
    ch_cos, ch_sin = _dft_tables(HEAD)
    eye_g = jnp.eye(FN_GROUPS, dtype=F32)
    ch_cs = jnp.concatenate([jnp.kron(eye_g, ch_cos), jnp.kron(eye_g, ch_sin)], axis=-1).astype(MXU_DT)

    c8 = jnp.zeros((8, d), F32).at[0:batch].set(c).at[batch].set(c_ctx)
    xs = jnp.concatenate([x.reshape(n_lat, d), ctx.reshape(batch * ctx_len, d)], axis=0)

    for layer in range(depth):
        need_ctx = layer < depth - 1
        rows_out = n_all if need_ctx else n_lat
        mod = _modulation(c8, w_mod, b_mod3, layer).reshape(8, 6, d)

        p = _ln_mod_matmul(xs, mod, w_in_big, layer, seg_of_tile=seg_of_tile, shift_row=0, tn=512)
        ps = _ln_mod_matmul(xs, mod, w_in_small, layer, seg_of_tile=seg_of_tile, shift_row=0, tn=HEAD)

        qkv = _dn_conv(p, conv_w, layer, blocks_per_seq=blocks_per_seq, n_lat_blocks=batch * blocks_per_seq)
        par = (jnp.zeros((8, HEAD), F32)
               .at[0, 2 * DN_HEADS:4 * DN_HEADS].set(dn_dt_bias[layer].reshape(-1))
               .at[1, 2 * DN_HEADS:4 * DN_HEADS].set(dn_a_log[layer].reshape(-1)))
        prep = _dn_prep(qkv, ps, par, batch=batch, blocks_per_seq=blocks_per_seq)
        o_f, o_b = _dn_scan(prep, batch=batch)
        a_dn = _dn_out(o_f, o_b, p, _pad_rows8(dn_norm[layer]), batch=batch, blocks_per_seq=blocks_per_seq,
                       nblk=rows_out // RB)

        qh, kh, vh = _attn_prep(p, rope_cos, rope_sin, _pad_rows8(attn_q_norm[layer]), _pad_rows8(attn_k_norm[layer]),
                                batch=batch, blocks_per_seq=blocks_per_seq)
        a_at = _attention(qh, kh, vh, lat_len=seq, with_ctx=need_ctx, n_rows=rows_out)

        y = _fnet_channel_dft(p, ch_cs)
        a_fn = _fnet_pos_dft(y, lat_cos, lat_sin, fnet_w_c, layer, batch=batch, seq_len=seq, row_off=0)
        a_fn = a_fn.reshape(n_lat, FN_W)
        if need_ctx:
            a_fn_c = _fnet_pos_dft(y, ctx_cos, ctx_sin, fnet_w_c, layer, batch=batch, seq_len=ctx_len, row_off=n_lat)
            a_fn = jnp.concatenate([a_fn, a_fn_c.reshape(batch * ctx_len, FN_W)], axis=0)

        xs = _out_proj(xs, a_dn, a_at, a_fn, w_out_c, mod, ln1_g3, ln1_b3, layer, seg_of_tile=seg_of_tile,
                       n_rows=rows_out, alpha=alpha)

        li = layer // 2
        if layer % 2 == 0:
            xs = _ffn(xs, mod, ffn_wg, ffn_wu, ffn_wd, ln2_g3, ln2_b3, layer, li, seg_of_tile=seg_of_tile,
                      n_rows=rows_out, alpha=alpha, tf=tf)
        else:
            xs = _moe_layer(xs, mod, router_p[li], moe_wg, moe_wu, moe_wd, ln2_g3, ln2_b3, layer, li,
                            seg_of_tile=seg_of_tile, n_rows=rows_out, alpha=alpha, tf=tf)
    return xs[:n_lat].reshape(batch, seq, d)
```

```python
import functools
import math

import jax
import jax.numpy as jnp
from jax import lax
from jax.experimental import pallas as pl
from jax.experimental.pallas import tpu as pltpu

F32 = jnp.float32
MXU_DT = jnp.bfloat16
HI = lax.Precision.HIGHEST

EPS = 1e-6
HEAD = 128
DN_HEADS = 8
AQ_HEADS = 4
AKV_HEADS = 2
FN_GROUPS = 4
CHUNK = 64
CONV_K = 5
GRID_W = 64
ROPE_THETA = 10000.0
N_EXPERTS = 8
DN_W = DN_HEADS * HEAD
AQ_W = AQ_HEADS * HEAD
AKV_W = AKV_HEADS * HEAD
FN_W = FN_GROUPS * HEAD
C_DNQ, C_DNK, C_DNV = 0, DN_W, 2 * DN_W
C_GATE = 3 * DN_W
C_AQ = 4 * DN_W
C_AK = C_AQ + AQ_W
C_AV = C_AK + AKV_W
C_FN = C_AV + AKV_W
P_COLS = C_FN + FN_W

TM = 512
RB = 256
DN_HPS = 2
TQ = 128
ATTN_KEY_CHUNK_CAP = 3072
V7X_VMEM_LIMIT = 56 * 1024 * 1024


def _cp(sem, vmem=V7X_VMEM_LIMIT):
    return pltpu.CompilerParams(dimension_semantics=sem, vmem_limit_bytes=vmem)


def _silu(x):
    return x * jax.nn.sigmoid(x)


def _ln(x):
    mu = jnp.mean(x, axis=-1, keepdims=True)
    xc = x - mu
    var = jnp.mean(xc * xc, axis=-1, keepdims=True)
    return xc * lax.rsqrt(var + EPS)


def _dot(a, b):
    return jnp.dot(a.astype(MXU_DT), b.astype(MXU_DT), preferred_element_type=F32)


def _dot_nt(a, b):
    return lax.dot_general(a.astype(MXU_DT), b.astype(MXU_DT), (((1,), (1,)), ((), ())),
                           preferred_element_type=F32)


def _mod_kernel(c_ref, w_ref, b_ref, o_ref):
    o_ref[...] = _dot(_silu(c_ref[...]), w_ref[...]) + b_ref[...]


def _modulation(c8, w_mod, b_mod, layer):
    d = c8.shape[1]
    n = w_mod.shape[2]
    tn = d // 2
    return pl.pallas_call(
        _mod_kernel,
        grid=(n // tn,),
        in_specs=[pl.BlockSpec((8, d), lambda j: (0, 0)),
                  pl.BlockSpec((None, d, tn), lambda j: (layer, 0, j)),
                  pl.BlockSpec((None, 1, tn), lambda j: (layer, 0, j))],
        out_specs=pl.BlockSpec((8, tn), lambda j: (0, j)),
        out_shape=jax.ShapeDtypeStruct((8, n), F32),
        compiler_params=_cp(("parallel",)),
        name="modulation",
    )(c8, w_mod, b_mod)


def _ln_mod_matmul_kernel(x_ref, mod_ref, w_ref, o_ref, h_ref, *, shift_row):
    @pl.when(pl.program_id(1) == 0)
    def _():
        h = _ln(x_ref[...]) * (1.0 + mod_ref[shift_row + 1:shift_row + 2, :]) + mod_ref[shift_row:shift_row + 1, :]
        h_ref[...] = h.astype(MXU_DT)

    o_ref[...] = jnp.dot(h_ref[...], w_ref[...], preferred_element_type=F32)


def _ln_mod_matmul(x, mod, w, layer, *, seg_of_tile, shift_row, tn):
    r, d = x.shape
    n = w.shape[2]
    return pl.pallas_call(
        functools.partial(_ln_mod_matmul_kernel, shift_row=shift_row),
        grid=(r // TM, n // tn),
        in_specs=[pl.BlockSpec((TM, d), lambda i, j: (i, 0)),
                  pl.BlockSpec((None, 6, d), lambda i, j: (seg_of_tile(i), 0, 0)),
                  pl.BlockSpec((None, d, tn), lambda i, j: (layer, 0, j))],
        out_specs=pl.BlockSpec((TM, tn), lambda i, j: (i, j)),
        out_shape=jax.ShapeDtypeStruct((r, n), F32),
        scratch_shapes=[pltpu.VMEM((TM, d), MXU_DT)],
        compiler_params=_cp(("parallel", "arbitrary")),
        name="in_proj",
    )(x, mod, w)


def _dn_conv_kernel(prev_ref, x_ref, next_ref, w_ref, o_ref, buf_ref, *, blocks_per_seq, n_lat_blocks):
    i = pl.program_id(0)
    part = pl.program_id(1)
    li = i % blocks_per_seq
    is_lat = i < n_lat_blocks
    prev_ok = jnp.logical_and(is_lat, li > 0)
    next_ok = jnp.logical_and(is_lat, li < blocks_per_seq - 1)
    buf_ref[0:8, :] = jnp.where(prev_ok, prev_ref[...], 0.0)
    buf_ref[8:8 + RB, :] = x_ref[...]
    buf_ref[8 + RB:16 + RB, :] = jnp.where(next_ok, next_ref[...], 0.0)
    pad = CONV_K // 2
    y = buf_ref[8 - pad:8 - pad + RB, :] * w_ref[0:1, :]
    for tap in range(1, CONV_K):
        y = y + buf_ref[8 - pad + tap:8 - pad + tap + RB, :] * w_ref[tap:tap + 1, :]
    y = _silu(y)
    normed = part < 2
    for h in range(DN_HEADS):
        yh = y[:, h * HEAD:(h + 1) * HEAD]
        inv = lax.rsqrt(jnp.sum(yh * yh, axis=-1, keepdims=True) + EPS)
        o_ref[:, h * HEAD:(h + 1) * HEAD] = yh * jnp.where(normed, inv, 1.0)


def _dn_conv(p, conv_w, layer, *, blocks_per_seq, n_lat_blocks):
    r = p.shape[0]
    nblk = r // RB
    sub = RB // 8
    last8 = r // 8 - 1
    return pl.pallas_call(
        functools.partial(_dn_conv_kernel, blocks_per_seq=blocks_per_seq, n_lat_blocks=n_lat_blocks),
        grid=(nblk, 3),
        in_specs=[pl.BlockSpec((8, DN_W), lambda i, c: (jnp.maximum(i * sub - 1, 0), c)),
                  pl.BlockSpec((RB, DN_W), lambda i, c: (i, c)),
                  pl.BlockSpec((8, DN_W), lambda i, c: (jnp.minimum((i + 1) * sub, last8), c)),
                  pl.BlockSpec((None, 8, DN_W), lambda i, c: (layer, 0, c))],
        out_specs=pl.BlockSpec((RB, DN_W), lambda i, c: (i, c)),
        out_shape=jax.ShapeDtypeStruct((r, 3 * DN_W), F32),
        scratch_shapes=[pltpu.VMEM((RB + 16, DN_W), F32)],
        compiler_params=_cp(("parallel", "parallel")),
        name="dn_conv",
    )(p, p, p, conv_w)


def _col(x, lane, idx):
    return jnp.sum(jnp.where(lane == idx, x, 0.0), axis=-1, keepdims=True)


def _dn_prep_kernel(q_ref, k_ref, v_ref, ps_ref, par_ref,
                    wq_f, u_f, kg_f, qk_f, dl_f, wq_b, u_b, kg_b, qk_b, dl_b):
    h0 = pl.program_id(1) * DN_HPS
    nchunk = RB // CHUNK
    ps = ps_ref[...]
    lane = lax.broadcasted_iota(jnp.int32, (RB, HEAD), 1)
    rowc = lax.broadcasted_iota(jnp.int32, (RB, HEAD), 0) % CHUNK
    beta_all = jax.nn.sigmoid(ps)
    z = ps + par_ref[0:1, :]
    dt = jnp.maximum(z, 0.0) + jnp.log(1.0 + jnp.exp(-jnp.abs(z)))
    g_all = -jnp.exp(par_ref[1:2, :]) * dt
    gf = g_all
    gb = g_all
    s = 1
    while s < CHUNK:
        gf = gf + jnp.where(rowc >= s, pltpu.roll(gf, s, 0), 0.0)
        gb = gb + jnp.where(rowc < CHUNK - s, pltpu.roll(gb, RB - s, 0), 0.0)
        s *= 2
    tot_all = gf + gb - g_all

    r2 = lax.broadcasted_iota(jnp.int32, (RB, RB), 0)
    c2 = lax.broadcasted_iota(jnp.int32, (RB, RB), 1)
    eye = (r2 == c2).astype(F32)
    same = (r2 // CHUNK) == (c2 // CHUNK)
    incl = (jnp.logical_and(same, r2 >= c2), jnp.logical_and(same, r2 <= c2))
    level_masks = []
    bs = 2
    while bs < CHUNK:
        level_masks.append(jnp.logical_and((r2 // (2 * bs)) == (c2 // (2 * bs)), (r2 // bs) != (c2 // bs)))
        bs *= 2

    probs = [(hh, d) for hh in range(DN_HPS) for d in range(2)]
    qs = [q_ref[:, hh * HEAD:(hh + 1) * HEAD] * (HEAD ** -0.5) for hh in range(DN_HPS)]
    ks = [k_ref[:, hh * HEAD:(hh + 1) * HEAD] for hh in range(DN_HPS)]
    vs = [v_ref[:, hh * HEAD:(hh + 1) * HEAD] for hh in range(DN_HPS)]
    kk = [_dot_nt(k, k) for k in ks]
    qk = [_dot_nt(q, k) for q, k in zip(qs, ks)]
    bcol, gcol, tcol, dm, lm, t = [], [], [], [], [], []
    for hh, d in probs:
        col = d * DN_HEADS + h0 + hh
        bcol.append(_col(beta_all, lane, col))
        gcol.append(_col(gf if d == 0 else gb, lane, 2 * DN_HEADS + col))
        tcol.append(_col(tot_all, lane, 2 * DN_HEADS + col))
        gi = jnp.broadcast_to(gcol[-1], (RB, RB))
        dm.append(jnp.where(incl[d], jnp.exp(jnp.minimum(gi - gi.T, 0.0)), 0.0))
        lm.append(jnp.where(r2 != c2, bcol[-1] * kk[hh] * dm[-1], 0.0))
        t.append(eye - jnp.where((r2 // 2) == (c2 // 2), lm[-1], 0.0))
    for mask in level_masks:
        tc = [_dot(ti, jnp.where(mask, li, 0.0)) for ti, li in zip(t, lm)]
        t = [ti - _dot(tci, ti) for ti, tci in zip(t, tc)]
    stores = []
    outs = ((wq_f, u_f, kg_f, qk_f, dl_f), (wq_b, u_b, kg_b, qk_b, dl_b))
    for p, (hh, d) in enumerate(probs):
        eg = jnp.exp(gcol[p])
        sol_u = _dot(t[p], vs[hh] * bcol[p])
        sol_w = _dot(t[p], ks[hh] * (bcol[p] * eg))
        qkm = jnp.where(incl[d], qk[hh] * dm[p], 0.0)
        q_g = qs[hh] * eg
        k_gt = (ks[hh] * jnp.exp(tcol[p] - gcol[p])).T
        dl = jnp.exp(tcol[p])
        stores.append((outs[d], hh, d, sol_w, q_g, sol_u, k_gt, qkm, dl))
    for (wq_ref, u_ref, kg_ref, qk_ref, dl_ref), hh, d, sol_w, q_g, sol_u, k_gt, qkm, dl in stores:
        for j in range(nchunk):
            slot = j if d == 0 else nchunk - 1 - j
            rows = slice(j * CHUNK, (j + 1) * CHUNK)
            wq_ref[hh, slot, 0:CHUNK, :] = sol_w[rows, :]
            wq_ref[hh, slot, CHUNK:2 * CHUNK, :] = q_g[rows, :]
            u_ref[hh, slot, :, :] = sol_u[rows, :]
            kg_ref[hh, slot, :, :] = k_gt[:, rows]
            qk_ref[hh, slot, :, :] = qkm[rows, rows]
            dl_ref[hh, slot, :, :] = jnp.broadcast_to(dl[j * CHUNK:j * CHUNK + 1, :], (8, HEAD))


def _dn_prep(qkv, ps, par, *, batch, blocks_per_seq):
    r = qkv.shape[0]
    nblk = r // RB
    n_lat = batch * blocks_per_seq
    nchunk = RB // CHUNK
    npos = (blocks_per_seq + 1) * nchunk
    chains = batch * DN_HEADS

    def bidx(i):
        return jnp.where(i < n_lat, i // blocks_per_seq, i - n_lat)

    def pos_f(i):
        return jnp.where(i < n_lat, 1 + i % blocks_per_seq, 0)

    def pos_b(i):
        return jnp.where(i < n_lat, blocks_per_seq - i % blocks_per_seq, 0)

    hsteps = DN_HEADS // DN_HPS
    wide = DN_HPS * HEAD

    def ospec(shape, pos):
        return pl.BlockSpec((DN_HPS, nchunk) + shape, lambda i, h: (bidx(i) * hsteps + h, pos(i), 0, 0))

    shapes = ((2 * CHUNK, HEAD), (CHUNK, HEAD), (HEAD, CHUNK), (CHUNK, CHUNK), (8, HEAD))
    out_specs = [ospec(s, pos_f) for s in shapes] + [ospec(s, pos_b) for s in shapes]
    out_shape = [jax.ShapeDtypeStruct((chains, npos) + s, F32) for s in shapes] * 2
    return pl.pallas_call(
        _dn_prep_kernel,
        grid=(nblk, hsteps),
        in_specs=[pl.BlockSpec((RB, wide), lambda i, h: (i, h)),
                  pl.BlockSpec((RB, wide), lambda i, h: (i, hsteps + h)),
                  pl.BlockSpec((RB, wide), lambda i, h: (i, 2 * hsteps + h)),
                  pl.BlockSpec((RB, HEAD), lambda i, h: (i, 0)),
                  pl.BlockSpec((8, HEAD), lambda i, h: (0, 0))],
        out_specs=out_specs,
        out_shape=out_shape,
        compiler_params=_cp(("parallel", "parallel")),
        name="dn_prep",
    )(qkv, qkv, qkv, ps, par)


def _dn_scan_kernel(wq_f, u_f, kg_f, qk_f, dl_f, wq_b, u_b, kg_b, qk_b, dl_b, o_f, o_b, s_ref):
    @pl.when(pl.program_id(1) == 0)
    def _():
        s_ref[...] = jnp.zeros_like(s_ref)

    ins = ((wq_f, u_f, kg_f, qk_f, dl_f, o_f), (wq_b, u_b, kg_b, qk_b, dl_b, o_b))
    chains = [(d, c) for d in range(2) for c in range(DN_HEADS)]
    state = [s_ref[d * DN_HEADS + c] for d, c in chains]
    r = [_dot(ins[d][0][c, 0], s) for (d, c), s in zip(chains, state)]
    v_new = [ins[d][1][c, 0] - ri[0:CHUNK, :] for (d, c), ri in zip(chains, r)]
    out = [ri[CHUNK:2 * CHUNK, :] + _dot(ins[d][3][c, 0], vi) for (d, c), ri, vi in zip(chains, r, v_new)]
    new_state = [s * ins[d][4][c, 0, 0:1, :] + _dot(ins[d][2][c, 0], vi)
                 for (d, c), s, vi in zip(chains, state, v_new)]
    for (d, c), oi, si in zip(chains, out, new_state):
        ins[d][5][c, 0] = oi
        s_ref[d * DN_HEADS + c] = si


def _dn_scan(prep, *, batch):
    chains, npos = prep[0].shape[:2]
    shapes = ((2 * CHUNK, HEAD), (CHUNK, HEAD), (HEAD, CHUNK), (CHUNK, CHUNK), (8, HEAD))
    spec = lambda s: pl.BlockSpec((DN_HEADS, 1) + s, lambda b, n: (b, n, 0, 0))
    in_specs = [spec(s) for s in shapes] * 2
    o_sds = jax.ShapeDtypeStruct((chains, npos, CHUNK, HEAD), F32)
    return pl.pallas_call(
        _dn_scan_kernel,
        grid=(batch, npos),
        in_specs=in_specs,
        out_specs=[spec((CHUNK, HEAD)), spec((CHUNK, HEAD))],
        out_shape=[o_sds, o_sds],
        scratch_shapes=[pltpu.VMEM((2 * DN_HEADS, HEAD, HEAD), F32)],
        compiler_params=_cp(("parallel", "arbitrary")),
        name="dn_scan",
    )(*prep)


def _dn_out_kernel(of_ref, ob_ref, gate_ref, nw_ref, o_ref):
    nchunk = RB // CHUNK
    nw = nw_ref[0:1, :]
    for h in range(DN_HEADS):
        for j in range(nchunk):
            o = of_ref[h, j] + ob_ref[h, nchunk - 1 - j]
            y = o * lax.rsqrt(jnp.mean(o * o, axis=-1, keepdims=True) + EPS) * nw
            gt = gate_ref[j * CHUNK:(j + 1) * CHUNK, h * HEAD:(h + 1) * HEAD]
            o_ref[j * CHUNK:(j + 1) * CHUNK, h * HEAD:(h + 1) * HEAD] = (y * _silu(gt)).astype(o_ref.dtype)


def _dn_out(o_f, o_b, p, norm_w, *, batch, blocks_per_seq, nblk):
    n_lat = batch * blocks_per_seq
    nchunk = RB // CHUNK
    bidx = lambda i: jnp.where(i < n_lat, i // blocks_per_seq, i - n_lat)
    pos_f = lambda i: jnp.where(i < n_lat, 1 + i % blocks_per_seq, 0)
    pos_b = lambda i: jnp.where(i < n_lat, blocks_per_seq - i % blocks_per_seq, 0)
    return pl.pallas_call(
        _dn_out_kernel,
        grid=(nblk,),
        in_specs=[pl.BlockSpec((DN_HEADS, nchunk, CHUNK, HEAD), lambda i: (bidx(i), pos_f(i), 0, 0)),
                  pl.BlockSpec((DN_HEADS, nchunk, CHUNK, HEAD), lambda i: (bidx(i), pos_b(i), 0, 0)),
                  pl.BlockSpec((RB, DN_W), lambda i: (i, C_GATE // DN_W)),
                  pl.BlockSpec((8, HEAD), lambda i: (0, 0))],
        out_specs=pl.BlockSpec((RB, DN_W), lambda i: (i, 0)),
        out_shape=jax.ShapeDtypeStruct((nblk * RB, DN_W), MXU_DT),
        compiler_params=_cp(("parallel",)),
        name="dn_out",
    )(o_f, o_b, p, norm_w)


def _attn_prep_kernel(q_ref, k_ref, v_ref, cos_ref, sin_ref, qw_ref, kw_ref, qo_ref, ko_ref, vo_ref, *, n_lat_blocks):
    is_lat = pl.program_id(0) < n_lat_blocks
    cos = jnp.where(is_lat, cos_ref[...], 1.0)
    sin = jnp.where(is_lat, sin_ref[...], 0.0)
    lane = lax.broadcasted_iota(jnp.int32, (RB, HEAD), 1)
    first_half = (lane % (HEAD // 2)) < (HEAD // 4)

    def norm_rope(x, w):
        y = x * lax.rsqrt(jnp.mean(x * x, axis=-1, keepdims=True) + EPS) * w
        swapped = jnp.where(first_half, pltpu.roll(y, HEAD - HEAD // 4, 1), pltpu.roll(y, HEAD // 4, 1))
        return y * cos + swapped * sin

    q_scale = HEAD ** -0.5 * math.log2(math.e)
    ones_col = jnp.where(lane == 0, 1.0, 0.0).astype(vo_ref.dtype)
    for h in range(AQ_HEADS):
        qh = norm_rope(q_ref[:, h * HEAD:(h + 1) * HEAD], qw_ref[0:1, :]) * q_scale
        qo_ref[0, h] = qh.astype(qo_ref.dtype)
    for h in range(AKV_HEADS):
        ko_ref[0, h] = norm_rope(k_ref[:, h * HEAD:(h + 1) * HEAD], kw_ref[0:1, :]).astype(ko_ref.dtype)
        vo_ref[0, h, :, 0:HEAD] = v_ref[:, h * HEAD:(h + 1) * HEAD].astype(vo_ref.dtype)
        vo_ref[0, h, :, HEAD:2 * HEAD] = ones_col


def _attn_prep(p, cos, sin, qw, kw, *, batch, blocks_per_seq):
    r = p.shape[0]
    nblk = r // RB
    n_lat = batch * blocks_per_seq
    ltot = (blocks_per_seq + 1) * RB
    bidx = lambda i: jnp.where(i < n_lat, i // blocks_per_seq, i - n_lat)
    lblk = lambda i: jnp.where(i < n_lat, i % blocks_per_seq, blocks_per_seq)
    tblk = lambda i: jnp.where(i < n_lat, i % blocks_per_seq, 0)
    return pl.pallas_call(
        functools.partial(_attn_prep_kernel, n_lat_blocks=n_lat),
        grid=(nblk,),
        in_specs=[pl.BlockSpec((RB, AQ_W), lambda i: (i, C_AQ // AQ_W)),
                  pl.BlockSpec((RB, AKV_W), lambda i: (i, C_AK // AKV_W)),
                  pl.BlockSpec((RB, AKV_W), lambda i: (i, C_AV // AKV_W)),
                  pl.BlockSpec((RB, HEAD), lambda i: (tblk(i), 0)),
                  pl.BlockSpec((RB, HEAD), lambda i: (tblk(i), 0)),
                  pl.BlockSpec((8, HEAD), lambda i: (0, 0)),
                  pl.BlockSpec((8, HEAD), lambda i: (0, 0))],
        out_specs=[pl.BlockSpec((1, AQ_HEADS, RB, HEAD), lambda i: (bidx(i), 0, lblk(i), 0)),
                   pl.BlockSpec((1, AKV_HEADS, RB, HEAD), lambda i: (bidx(i), 0, lblk(i), 0)),
                   pl.BlockSpec((1, AKV_HEADS, RB, 2 * HEAD), lambda i: (bidx(i), 0, lblk(i), 0))],
        out_shape=[jax.ShapeDtypeStruct((batch, AQ_HEADS, ltot, HEAD), MXU_DT),
                   jax.ShapeDtypeStruct((batch, AKV_HEADS, ltot, HEAD), MXU_DT),
                   jax.ShapeDtypeStruct((batch, AKV_HEADS, ltot, 2 * HEAD), MXU_DT)],
        compiler_params=_cp(("parallel",)),
        name="attn_prep",
    )(p, p, p, cos, sin, qw, kw)


def _key_chunk(ltot, cap):
    return max(kc for kc in range(HEAD, cap + 1, HEAD) if ltot % kc == 0)


def _attn_kernel(q_ref, k_ref, v_ref, o_ref, *, n_lat_qblocks, lat_len, key_chunk):
    grp = AQ_HEADS // AKV_HEADS

    def attend(start, stop, kc):
        q = q_ref[0].reshape(grp * TQ, HEAD)
        m = None
        acc = None
        for c0 in range(start, stop, kc):
            s = lax.dot_general(q, k_ref[0, 0, c0:c0 + kc, :], (((1,), (1,)), ((), ())), preferred_element_type=F32)
            m_c = jnp.max(s, axis=-1, keepdims=True)
            m_new = m_c if m is None else jnp.maximum(m, m_c)
            pv = jnp.dot(jnp.exp2(s - m_new).astype(MXU_DT), v_ref[0, 0, c0:c0 + kc, :], preferred_element_type=F32)
            acc = pv if m is None else acc * jnp.exp2(m - m_new) + pv
            m = m_new
        o = acc[:, 0:HEAD] / acc[:, HEAD:HEAD + 1]
        for g in range(grp):
            o_ref[:, g * HEAD:(g + 1) * HEAD] = o[g * TQ:(g + 1) * TQ, :].astype(o_ref.dtype)

    qi = pl.program_id(2)
    ltot = k_ref.shape[2]

    @pl.when(qi < n_lat_qblocks)
    def _():
        attend(0, ltot, key_chunk)

    @pl.when(qi >= n_lat_qblocks)
    def _():
        attend(lat_len, ltot, ltot - lat_len)


def _attention(qh, kh, vh, *, lat_len, with_ctx, n_rows):
    batch, _, ltot, _ = qh.shape
    grp = AQ_HEADS // AKV_HEADS
    n_lat_q = lat_len // TQ
    n_q = ltot // TQ if with_ctx else n_lat_q
    n_ctx_q = (ltot - lat_len) // TQ

    def orow(b, qi):
        return jnp.where(qi < n_lat_q, b * n_lat_q + qi, batch * n_lat_q + b * n_ctx_q + (qi - n_lat_q))

    return pl.pallas_call(
        functools.partial(_attn_kernel, n_lat_qblocks=n_lat_q, lat_len=lat_len,
                          key_chunk=_key_chunk(ltot, ATTN_KEY_CHUNK_CAP)),
        grid=(batch, AKV_HEADS, n_q),
        in_specs=[pl.BlockSpec((1, grp, TQ, HEAD), lambda b, g, qi: (b, g, qi, 0)),
                  pl.BlockSpec((1, 1, ltot, HEAD), lambda b, g, qi: (b, g, 0, 0)),
                  pl.BlockSpec((1, 1, ltot, 2 * HEAD), lambda b, g, qi: (b, g, 0, 0))],
        out_specs=pl.BlockSpec((TQ, grp * HEAD), lambda b, g, qi: (orow(b, qi), g)),
        out_shape=jax.ShapeDtypeStruct((n_rows, AQ_W), MXU_DT),
        compiler_params=_cp(("parallel", "parallel", "arbitrary")),
        name="attention",
    )(qh, kh, vh)


def _plain_matmul_kernel(x_ref, w_ref, o_ref):
    o_ref[...] = _dot(x_ref[...], w_ref[...]).astype(o_ref.dtype)


def _fnet_channel_dft(p, cs):
    r = p.shape[0]
    return pl.pallas_call(
        _plain_matmul_kernel,
        grid=(r // TM,),
        in_specs=[pl.BlockSpec((TM, FN_W), lambda i: (i, C_FN // FN_W)),
                  pl.BlockSpec((FN_W, 2 * FN_W), lambda i: (0, 0))],
        out_specs=pl.BlockSpec((TM, 2 * FN_W), lambda i: (i, 0)),
        out_shape=jax.ShapeDtypeStruct((r, 2 * FN_W), MXU_DT),
        compiler_params=_cp(("parallel",)),
        name="fnet_channel_dft",
    )(p, cs)


def _fnet_pos_kernel(*refs, batch):
    cos_ref, sin_ref = refs[0], refs[1]
    y_refs = refs[2:2 + batch]
    w_ref = refs[2 + batch]
    o_ref = refs[3 + batch]
    acc_ref = refs[4 + batch]
    kk = pl.program_id(1)

    @pl.when(kk == 0)
    def _():
        acc_ref[...] = jnp.zeros_like(acc_ref)

    c = cos_ref[...]
    s = sin_ref[...]
    for b in range(batch):
        y = y_refs[b][...]
        acc_ref[b] += (jnp.dot(c, y[:, 0:FN_W], preferred_element_type=F32)
                       - jnp.dot(s, y[:, FN_W:2 * FN_W], preferred_element_type=F32))

    @pl.when(kk == pl.num_programs(1) - 1)
    def _():
        for b in range(batch):
            o_ref[b] = _dot(acc_ref[b], w_ref[...]).astype(o_ref.dtype)


def _fnet_pos_dft(y, cos_t, sin_t, fnet_w, layer, *, batch, seq_len, row_off):
    t = min(1024, seq_len)
    nt = seq_len // t
    y_specs = [pl.BlockSpec((t, 2 * FN_W), functools.partial(lambda i, k, b: (row_off // t + b * nt + k, 0), b=b))
               for b in range(batch)]
    return pl.pallas_call(
        functools.partial(_fnet_pos_kernel, batch=batch),
        grid=(nt, nt),
        in_specs=[pl.BlockSpec((t, t), lambda i, k: (i, k)),
                  pl.BlockSpec((t, t), lambda i, k: (i, k))] + y_specs +
                 [pl.BlockSpec((None, FN_W, FN_W), lambda i, k: (layer, 0, 0))],
        out_specs=pl.BlockSpec((batch, t, FN_W), lambda i, k: (0, i, 0)),
        out_shape=jax.ShapeDtypeStruct((batch, seq_len, FN_W), MXU_DT),
        scratch_shapes=[pltpu.VMEM((batch, t, FN_W), F32)],
        compiler_params=_cp(("parallel", "arbitrary")),
        name="fnet_pos_dft",
    )(cos_t, sin_t, *([y] * batch), fnet_w)


def _dft_tables(n):
    idx = jnp.arange(n, dtype=jnp.int32)
    ang = ((idx[:, None] * idx[None, :]) % n).astype(F32) * (2.0 * math.pi / n)
    scale = n ** -0.5
    return jnp.cos(ang) * scale, jnp.sin(ang) * scale


def _fft_stage1_kernel(y_ref, c_ref, s_ref, twc_ref, tws_ref, o_ref, *, j1_per_step):
    lane = lax.broadcasted_iota(jnp.int32, (HEAD, HEAD), 1)
    c = c_ref[...]
    s = s_ref[...]
    for t in range(j1_per_step):
        j1 = pl.program_id(1) * j1_per_step + t
        yc = y_ref[:, t * 2 * FN_W:t * 2 * FN_W + FN_W]
        ys = y_ref[:, t * 2 * FN_W + FN_W:(t + 1) * 2 * FN_W]
        ar = jnp.dot(c, yc, preferred_element_type=F32) - jnp.dot(s, ys, preferred_element_type=F32)
        ai = -(jnp.dot(s, yc, preferred_element_type=F32) + jnp.dot(c, ys, preferred_element_type=F32))
        tc = _col(twc_ref[...], lane, j1)
        ts = _col(tws_ref[...], lane, j1)
        o_ref[t, :, 0:FN_W] = (ar * tc + ai * ts).astype(o_ref.dtype)
        o_ref[t, :, FN_W:2 * FN_W] = (ai * tc - ar * ts).astype(o_ref.dtype)


def _fft_stage2_kernel(b_ref, c_ref, s_ref, w_ref, o_ref, *, k2_per_step):
    c = c_ref[...]
    s = s_ref[...]
    for t in range(k2_per_step):
        br = b_ref[:, t * 2 * FN_W:t * 2 * FN_W + FN_W]
        bi = b_ref[:, t * 2 * FN_W + FN_W:(t + 1) * 2 * FN_W]
        xr = jnp.dot(c, br, preferred_element_type=F32) + jnp.dot(s, bi, preferred_element_type=F32)
        o_ref[:, t * FN_W:(t + 1) * FN_W] = _dot(xr, w_ref[...]).astype(o_ref.dtype)


def _fnet_pos_fft(y, tabs, fnet_w, layer, *, batch, seq_len):
    c2, s2, twc, tws, c1, s1 = tabs
    l1 = seq_len // HEAD
    per = min(8, l1)
    yv = y.reshape(y.shape[0] // l1, l1 * 2 * FN_W)
    b_arr = pl.pallas_call(
        functools.partial(_fft_stage1_kernel, j1_per_step=per),
        grid=(batch, l1 // per),
        in_specs=[pl.BlockSpec((HEAD, per * 2 * FN_W), lambda b, j: (b, j)),
                  pl.BlockSpec((HEAD, HEAD), lambda b, j: (0, 0)),
                  pl.BlockSpec((HEAD, HEAD), lambda b, j: (0, 0)),
                  pl.BlockSpec((HEAD, HEAD), lambda b, j: (0, 0)),
                  pl.BlockSpec((HEAD, HEAD), lambda b, j: (0, 0))],
        out_specs=pl.BlockSpec((per, HEAD, 2 * FN_W), lambda b, j: (b * (l1 // per) + j, 0, 0)),
        out_shape=jax.ShapeDtypeStruct((batch * l1, HEAD, 2 * FN_W), MXU_DT),
        compiler_params=_cp(("parallel", "parallel")),
        name="fnet_fft_stage1",
    )(yv, c2, s2, twc, tws)
    bv = b_arr.reshape(batch, l1, HEAD * 2 * FN_W)
    out = pl.pallas_call(
        functools.partial(_fft_stage2_kernel, k2_per_step=8),
        grid=(batch, HEAD // 8),
        in_specs=[pl.BlockSpec((None, l1, 8 * 2 * FN_W), lambda b, k: (b, 0, k)),
                  pl.BlockSpec((l1, l1), lambda b, k: (0, 0)),
                  pl.BlockSpec((l1, l1), lambda b, k: (0, 0)),
                  pl.BlockSpec((None, FN_W, FN_W), lambda b, k: (layer, 0, 0))],
        out_specs=pl.BlockSpec((None, l1, 8 * FN_W), lambda b, k: (b, 0, k)),
        out_shape=jax.ShapeDtypeStruct((batch, l1, HEAD * FN_W), MXU_DT),
        compiler_params=_cp(("parallel", "parallel")),
        name="fnet_fft_stage2",
    )(bv, c1, s1, fnet_w)
    return out.reshape(batch * seq_len, FN_W)


def _fft_tables(seq_len):
    l1 = seq_len // HEAD
    c2, s2 = _dft_tables(HEAD)
    c1, s1 = _dft_tables(l1)
    k2 = jnp.arange(HEAD, dtype=jnp.int32)[:, None]
    j1 = jnp.arange(HEAD, dtype=jnp.int32)[None, :]
    ang = ((k2 * j1) % seq_len).astype(F32) * (2.0 * math.pi / seq_len)
    return (c2.astype(MXU_DT), s2.astype(MXU_DT), jnp.cos(ang), jnp.sin(ang), c1.astype(MXU_DT), s1.astype(MXU_DT))


def _out_proj_kernel(x_ref, dn_ref, at_ref, fn_ref, w_ref, mod_ref, g_ref, b_ref, o_ref, *, alpha):
    acc = jnp.dot(dn_ref[...], w_ref[0:DN_W, :], preferred_element_type=F32)
    acc += jnp.dot(at_ref[...], w_ref[DN_W:DN_W + AQ_W, :], preferred_element_type=F32)
    acc += jnp.dot(fn_ref[...], w_ref[DN_W + AQ_W:, :], preferred_element_type=F32)
    y = alpha * x_ref[...] + mod_ref[2:3, :] * acc
    o_ref[...] = _ln(y) * g_ref[0:1, :] + b_ref[0:1, :]


def _out_proj(x, a_dn, a_at, a_fn, w_out, mod, g, b, layer, *, seg_of_tile, n_rows, alpha):
    d = x.shape[1]
    tm = RB
    seg = lambda i: seg_of_tile(i // (TM // tm))
    return pl.pallas_call(
        functools.partial(_out_proj_kernel, alpha=alpha),
        grid=(n_rows // tm,),
        in_specs=[pl.BlockSpec((tm, d), lambda i: (i, 0)),
                  pl.BlockSpec((tm, DN_W), lambda i: (i, 0)),
                  pl.BlockSpec((tm, AQ_W), lambda i: (i, 0)),
                  pl.BlockSpec((tm, FN_W), lambda i: (i, 0)),
                  pl.BlockSpec((None, DN_W + AQ_W + FN_W, d), lambda i: (layer, 0, 0)),
                  pl.BlockSpec((None, 6, d), lambda i: (seg(i), 0, 0)),
                  pl.BlockSpec((None, 1, d), lambda i: (layer, 0, 0)),
                  pl.BlockSpec((None, 1, d), lambda i: (layer, 0, 0))],
        out_specs=pl.BlockSpec((tm, d), lambda i: (i, 0)),
        out_shape=jax.ShapeDtypeStruct((n_rows, d), F32),
        compiler_params=_cp(("parallel",)),
        name="out_proj",
    )(x, a_dn, a_at, a_fn, w_out, mod, g, b)


def _ffn_kernel(x_ref, mod_ref, wg_ref, wu_ref, wd_ref, g_ref, b_ref, o_ref, h_ref, acc_ref, *, alpha):
    f = pl.program_id(1)

    @pl.when(f == 0)
    def _():
        h = _ln(x_ref[...]) * (1.0 + mod_ref[4:5, :]) + mod_ref[3:4, :]
        h_ref[...] = h.astype(MXU_DT)
        acc_ref[...] = jnp.zeros_like(acc_ref)

    h = h_ref[...]
    gate = jnp.dot(h, wg_ref[...], preferred_element_type=F32)
    up = jnp.dot(h, wu_ref[...], preferred_element_type=F32)
    acc_ref[...] += jnp.dot((_silu(gate) * up).astype(MXU_DT), wd_ref[...], preferred_element_type=F32)

    @pl.when(f == pl.num_programs(1) - 1)
    def _():
        y = alpha * x_ref[...] + mod_ref[5:6, :] * acc_ref[...]
        o_ref[...] = _ln(y) * g_ref[0:1, :] + b_ref[0:1, :]


def _ffn(x, mod, wg, wu, wd, g, b, layer, li, *, seg_of_tile, n_rows, alpha, tf):
    d = x.shape[1]
    ff = wg.shape[2]
    return pl.pallas_call(
        functools.partial(_ffn_kernel, alpha=alpha),
        grid=(n_rows // TM, ff // tf),
        in_specs=[pl.BlockSpec((TM, d), lambda i, f: (i, 0)),
                  pl.BlockSpec((None, 6, d), lambda i, f: (seg_of_tile(i), 0, 0)),
                  pl.BlockSpec((None, d, tf), lambda i, f: (li, 0, f)),
                  pl.BlockSpec((None, d, tf), lambda i, f: (li, 0, f)),
                  pl.BlockSpec((None, tf, d), lambda i, f: (li, f, 0)),
                  pl.BlockSpec((None, 1, d), lambda i, f: (layer, 0, 0)),
                  pl.BlockSpec((None, 1, d), lambda i, f: (layer, 0, 0))],
        out_specs=pl.BlockSpec((TM, d), lambda i, f: (i, 0)),
        out_shape=jax.ShapeDtypeStruct((n_rows, d), F32),
        scratch_shapes=[pltpu.VMEM((TM, d), MXU_DT), pltpu.VMEM((TM, d), F32)],
        compiler_params=_cp(("parallel", "arbitrary")),
        name="ffn",
    )(x, mod, wg, wu, wd, g, b)


def _router_kernel(x_ref, mod_ref, rw_ref, h_ref, idx_ref, gate_ref):
    h = _ln(x_ref[...]) * (1.0 + mod_ref[4:5, :]) + mod_ref[3:4, :]
    h_ref[...] = h.astype(h_ref.dtype)
    logits = jnp.dot(h, rw_ref[...], precision=HI, preferred_element_type=F32)
    lane = lax.broadcasted_iota(jnp.int32, logits.shape, 1)
    neg = jnp.float32(-jnp.inf)
    l1 = jnp.where(lane < N_EXPERTS, logits, neg)
    m1 = jnp.max(l1, axis=-1, keepdims=True)
    i1 = jnp.min(jnp.where(l1 == m1, lane, HEAD), axis=-1, keepdims=True)
    l2 = jnp.where(lane == i1, neg, l1)
    m2 = jnp.max(l2, axis=-1, keepdims=True)
    i2 = jnp.min(jnp.where(l2 == m2, lane, HEAD), axis=-1, keepdims=True)
    e = jnp.exp(m2 - m1)
    g1 = 1.0 / (1.0 + e)
    g2 = e / (1.0 + e)
    idx_ref[...] = jnp.where(lane == 0, i1, jnp.where(lane == 1, i2, 0))
    gate_ref[...] = jnp.where(lane == 0, g1, jnp.where(lane == 1, g2, 0.0))


def _router(x, mod, rw, *, seg_of_tile, n_rows):
    d = x.shape[1]
    return pl.pallas_call(
        _router_kernel,
        grid=(n_rows // TM,),
        in_specs=[pl.BlockSpec((TM, d), lambda i: (i, 0)),
                  pl.BlockSpec((None, 6, d), lambda i: (seg_of_tile(i), 0, 0)),
                  pl.BlockSpec((d, HEAD), lambda i: (0, 0))],
        out_specs=[pl.BlockSpec((TM, d), lambda i: (i, 0)),
                   pl.BlockSpec((TM, HEAD), lambda i: (i, 0)),
                   pl.BlockSpec((TM, HEAD), lambda i: (i, 0))],
        out_shape=[jax.ShapeDtypeStruct((n_rows, d), MXU_DT),
                   jax.ShapeDtypeStruct((n_rows, HEAD), jnp.int32),
                   jax.ShapeDtypeStruct((n_rows, HEAD), F32)],
        compiler_params=_cp(("parallel",)),
        name="router",
    )(x, mod, rw)


def _moe_kernel(te_ref, nv_ref, h_ref, wg_ref, wu_ref, wd_ref, o_ref, acc_ref):
    t = pl.program_id(0)
    f = pl.program_id(1)
    last = pl.num_programs(1) - 1
    valid = t < nv_ref[0]

    @pl.when(jnp.logical_and(valid, f == 0))
    def _():
        acc_ref[...] = jnp.zeros_like(acc_ref)

    @pl.when(valid)
    def _():
        h = h_ref[...]
        gate = jnp.dot(h, wg_ref[...], preferred_element_type=F32)
        up = jnp.dot(h, wu_ref[...], preferred_element_type=F32)
        acc_ref[...] += jnp.dot((_silu(gate) * up).astype(MXU_DT), wd_ref[...], preferred_element_type=F32)

    @pl.when(jnp.logical_and(valid, f == last))
    def _():
        o_ref[...] = acc_ref[...]

    @pl.when(jnp.logical_and(jnp.logical_not(valid), f == last))
    def _():
        o_ref[...] = jnp.zeros_like(o_ref)


def _moe_experts(tile_expert, n_valid, h_sorted, wg, wu, wd, mi, *, tf):
    mp, d = h_sorted.shape
    ff = wg.shape[3]
    nf = ff // tf

    def fidx(t, f, nv):
        return jnp.where(t < nv[0], f, nf - 1)

    grid_spec = pltpu.PrefetchScalarGridSpec(
        num_scalar_prefetch=2,
        grid=(mp // TM, nf),
        in_specs=[pl.BlockSpec((TM, d), lambda t, f, te, nv: (t, 0)),
                  pl.BlockSpec((None, None, d, tf), lambda t, f, te, nv: (mi, te[t], 0, fidx(t, f, nv))),
                  pl.BlockSpec((None, None, d, tf), lambda t, f, te, nv: (mi, te[t], 0, fidx(t, f, nv))),
                  pl.BlockSpec((None, None, tf, d), lambda t, f, te, nv: (mi, te[t], fidx(t, f, nv), 0))],
        out_specs=pl.BlockSpec((TM, d), lambda t, f, te, nv: (t, 0)),
        scratch_shapes=[pltpu.VMEM((TM, d), F32)],
    )
    return pl.pallas_call(
        _moe_kernel,
        grid_spec=grid_spec,
        out_shape=jax.ShapeDtypeStruct((mp, d), F32),
        compiler_params=_cp(("arbitrary", "arbitrary")),
        name="moe_experts",
    )(tile_expert, n_valid, h_sorted, wg, wu, wd)


def _moe_combine_kernel(x_ref, y1_ref, y2_ref, gate_ref, mod_ref, g_ref, b_ref, o_ref, *, alpha):
    gates = gate_ref[...]
    y = gates[:, 0:1] * y1_ref[...] + gates[:, 1:2] * y2_ref[...]
    z = alpha * x_ref[...] + mod_ref[5:6, :] * y
    o_ref[...] = _ln(z) * g_ref[0:1, :] + b_ref[0:1, :]


def _moe_combine(x, y1, y2, gates, mod, g, b, layer, *, seg_of_tile, n_rows, alpha):
    d = x.shape[1]
    tm = RB
    seg = lambda i: seg_of_tile(i // (TM // tm))
    return pl.pallas_call(
        functools.partial(_moe_combine_kernel, alpha=alpha),
        grid=(n_rows // tm,),
        in_specs=[pl.BlockSpec((tm, d), lambda i: (i, 0)),
                  pl.BlockSpec((tm, d), lambda i: (i, 0)),
                  pl.BlockSpec((tm, d), lambda i: (i, 0)),
                  pl.BlockSpec((tm, HEAD), lambda i: (i, 0)),
                  pl.BlockSpec((None, 6, d), lambda i: (seg(i), 0, 0)),
                  pl.BlockSpec((None, 1, d), lambda i: (layer, 0, 0)),
                  pl.BlockSpec((None, 1, d), lambda i: (layer, 0, 0))],
        out_specs=pl.BlockSpec((tm, d), lambda i: (i, 0)),
        out_shape=jax.ShapeDtypeStruct((n_rows, d), F32),
        compiler_params=_cp(("parallel",)),
        name="moe_combine",
    )(x, y1, y2, gates, mod, g, b)


def _moe_layer(x, mod, rw, wg, wu, wd, g, b, layer, mi, *, seg_of_tile, n_rows, alpha, tf):
    h, idx, gates = _router(x, mod, rw, seg_of_tile=seg_of_tile, n_rows=n_rows)
    na = 2 * n_rows
    mp = na + N_EXPERTS * TM
    e_flat = idx[:, 0:2].reshape(na)
    onehot = (e_flat[:, None] == jnp.arange(N_EXPERTS, dtype=jnp.int32)[None, :]).astype(jnp.int32)
    csum = jnp.cumsum(onehot, axis=0)
    rank = jnp.sum((csum - onehot) * onehot, axis=1)
    counts = csum[-1]
    padded = ((counts + TM - 1) // TM) * TM
    ends = jnp.cumsum(padded)
    dest = (ends - padded)[e_flat] + rank
    src_token = jnp.zeros((mp,), jnp.int32).at[dest].set(jnp.arange(na, dtype=jnp.int32) // 2)
    tile_start = jnp.arange(mp // TM, dtype=jnp.int32) * TM
    tile_expert = jnp.sum((ends[None, :] <= tile_start[:, None]).astype(jnp.int32), axis=1)
    tile_expert = jnp.minimum(tile_expert, N_EXPERTS - 1)
    n_valid = (ends[-1] // TM).astype(jnp.int32).reshape(1)
    h_sorted = jnp.take(h, src_token, axis=0)
    y_sorted = _moe_experts(tile_expert, n_valid, h_sorted, wg, wu, wd, mi, tf=tf)
    dest2 = dest.reshape(n_rows, 2)
    y1 = jnp.take(y_sorted, dest2[:, 0], axis=0)
    y2 = jnp.take(y_sorted, dest2[:, 1], axis=0)
    return _moe_combine(x, y1, y2, gates, mod, g, b, layer, seg_of_tile=seg_of_tile, n_rows=n_rows, alpha=alpha)


def _rope_tables(seq_len):
    axis_dim = HEAD // 2
    inv = ROPE_THETA ** (-jnp.arange(0, axis_dim, 2, dtype=F32) / axis_dim)
    t = jnp.arange(seq_len, dtype=jnp.int32)
    row = (t // GRID_W).astype(F32)[:, None] * inv
    col = (t % GRID_W).astype(F32)[:, None] * inv
    cos = jnp.concatenate([jnp.cos(row), jnp.cos(row), jnp.cos(col), jnp.cos(col)], axis=-1)
    sin = jnp.concatenate([-jnp.sin(row), jnp.sin(row), -jnp.sin(col), jnp.sin(col)], axis=-1)
    return cos, sin


def _pad_rows8(v):
    return jnp.zeros((8, v.shape[-1]), F32).at[0].set(v.astype(F32))


def kernel(x, c, ctx, c_ctx, w_mod, b_mod, w_in, dn_conv, dn_a_log, dn_dt_bias, dn_norm, attn_q_norm, attn_k_norm,
           fnet_w, w_out, ln1_g, ln1_b, ln2_g, ln2_b, ffn_w_gate, ffn_w_up, ffn_w_down, router, moe_w_gate,
           moe_w_up, moe_w_down):
    batch, seq, d = x.shape
    ctx_len = ctx.shape[1]
    depth = w_mod.shape[0]
    ff = ffn_w_gate.shape[2]
    assert ctx_len == RB and seq % TM == 0 and (batch * ctx_len) % TM == 0 and TM % RB == 0
    alpha = (2 * depth) ** 0.25
    n_lat = batch * seq
    n_all = n_lat + batch * ctx_len
    tiles_per_seq = seq // TM
    blocks_per_seq = seq // RB
    seg_of_tile = lambda i: jnp.minimum(i // tiles_per_seq, batch)
    tf = 512 if ff % 512 == 0 else ff

    w_in_big = jnp.concatenate([w_in[:, :, :4 * DN_W], w_in[:, :, 4 * DN_W + 4 * DN_HEADS:]], axis=-1).astype(MXU_DT)
    w_in_small = jnp.pad(w_in[:, :, 4 * DN_W:4 * DN_W + 4 * DN_HEADS],
                         ((0, 0), (0, 0), (0, HEAD - 4 * DN_HEADS))).astype(MXU_DT)
    conv_w = jnp.pad(dn_conv, ((0, 0), (0, 8 - CONV_K), (0, 0)))
    w_out_c = w_out.astype(MXU_DT)
    fnet_w_c = fnet_w.astype(MXU_DT)
    ffn_wg, ffn_wu, ffn_wd = ffn_w_gate.astype(MXU_DT), ffn_w_up.astype(MXU_DT), ffn_w_down.astype(MXU_DT)
    moe_wg, moe_wu, moe_wd = moe_w_gate.astype(MXU_DT), moe_w_up.astype(MXU_DT), moe_w_down.astype(MXU_DT)
    router_p = jnp.pad(router, ((0, 0), (0, 0), (0, HEAD - N_EXPERTS)))
    b_mod3 = b_mod[:, None, :]
    ln1_g3, ln1_b3, ln2_g3, ln2_b3 = ln1_g[:, None, :], ln1_b[:, None, :], ln2_g[:, None, :], ln2_b[:, None, :]

    rope_cos, rope_sin = _rope_tables(seq)
    lat_tabs = _fft_tables(seq)
    ctx_cos, ctx_sin = (t.astype(MXU_DT) for t in _dft_tables(ctx_len))
    ch_cos, ch_sin = _dft_tables(HEAD)
    eye_g = jnp.eye(FN_GROUPS, dtype=F32)
    ch_cs = jnp.concatenate([jnp.kron(eye_g, ch_cos), jnp.kron(eye_g, ch_sin)], axis=-1).astype(MXU_DT)

    c8 = jnp.zeros((8, d), F32).at[0:batch].set(c).at[batch].set(c_ctx)
    xs = jnp.concatenate([x.reshape(n_lat, d), ctx.reshape(batch * ctx_len, d)], axis=0)

    for layer in range(depth):
        need_ctx = layer < depth - 1
        rows_out = n_all if need_ctx else n_lat
        mod = _modulation(c8, w_mod, b_mod3, layer).reshape(8, 6, d)

        p = _ln_mod_matmul(xs, mod, w_in_big, layer, seg_of_tile=seg_of_tile, shift_row=0, tn=P_COLS // 4)
        ps = _ln_mod_matmul(xs, mod, w_in_small, layer, seg_of_tile=seg_of_tile, shift_row=0, tn=HEAD)

        qkv = _dn_conv(p, conv_w, layer, blocks_per_seq=blocks_per_seq, n_lat_blocks=batch * blocks_per_seq)
        par = (jnp.zeros((8, HEAD), F32)
               .at[0, 2 * DN_HEADS:4 * DN_HEADS].set(dn_dt_bias[layer].reshape(-1))
               .at[1, 2 * DN_HEADS:4 * DN_HEADS].set(dn_a_log[layer].reshape(-1)))
        prep = _dn_prep(qkv, ps, par, batch=batch, blocks_per_seq=blocks_per_seq)
        o_f, o_b = _dn_scan(prep, batch=batch)
        a_dn = _dn_out(o_f, o_b, p, _pad_rows8(dn_norm[layer]), batch=batch, blocks_per_seq=blocks_per_seq,
                       nblk=rows_out // RB)

        qh, kh, vh = _attn_prep(p, rope_cos, rope_sin, _pad_rows8(attn_q_norm[layer]), _pad_rows8(attn_k_norm[layer]),
                                batch=batch, blocks_per_seq=blocks_per_seq)
        a_at = _attention(qh, kh, vh, lat_len=seq, with_ctx=need_ctx, n_rows=rows_out)

        y = _fnet_channel_dft(p, ch_cs)
        a_fn = _fnet_pos_fft(y, lat_tabs, fnet_w_c, layer, batch=batch, seq_len=seq)
        if need_ctx:
            a_fn_c = _fnet_pos_dft(y, ctx_cos, ctx_sin, fnet_w_c, layer, batch=batch, seq_len=ctx_len, row_off=n_lat)
            a_fn = jnp.concatenate([a_fn, a_fn_c.reshape(batch * ctx_len, FN_W)], axis=0)

        xs = _out_proj(xs, a_dn, a_at, a_fn, w_out_c, mod, ln1_g3, ln1_b3, layer, seg_of_tile=seg_of_tile,
                       n_rows=rows_out, alpha=alpha)

        li = layer // 2
        if layer % 2 == 0:
            xs = _ffn(xs, mod, ffn_wg, ffn_wu, ffn_wd, ln2_g3, ln2_b3, layer, li, seg_of_tile=seg_of_tile,
                      n_rows=rows_out, alpha=alpha, tf=tf)
        else:
            xs = _moe_layer(xs, mod, router_p[li], moe_wg, moe_wu, moe_wd, ln2_g3, ln2_b3, layer, li,
                            seg_of_tile=seg_of_tile, n_rows=rows_out, alpha=alpha, tf=tf)
    return xs[:n_lat].reshape(batch, seq, d)
```

```python
import functools
import math

import jax
import jax.numpy as jnp
from jax import lax
from jax.experimental import pallas as pl
from jax.experimental.pallas import tpu as pltpu

F32 = jnp.float32
MXU_DT = jnp.bfloat16
HI = lax.Precision.HIGHEST

EPS = 1e-6
HEAD = 128
DN_HEADS = 8
AQ_HEADS = 4
AKV_HEADS = 2
FN_GROUPS = 4
CHUNK = 64
CONV_K = 5
GRID_W = 64
ROPE_THETA = 10000.0
N_EXPERTS = 8
DN_W = DN_HEADS * HEAD
AQ_W = AQ_HEADS * HEAD
AKV_W = AKV_HEADS * HEAD
FN_W = FN_GROUPS * HEAD
C_DNQ, C_DNK, C_DNV = 0, DN_W, 2 * DN_W
C_GATE = 3 * DN_W
C_AQ = 4 * DN_W
C_AK = C_AQ + AQ_W
C_AV = C_AK + AKV_W
C_FN = C_AV + AKV_W
P_COLS = C_FN + FN_W

TM = 512
TM_MOE = 768
TF_MOE = 256
RB = 256
DN_HPS = 2
TQ = 256
ATTN_KEY_CHUNK_CAP = 3072
V7X_VMEM_LIMIT = 56 * 1024 * 1024


def _cp(sem, vmem=V7X_VMEM_LIMIT):
    return pltpu.CompilerParams(dimension_semantics=sem, vmem_limit_bytes=vmem)


def _silu(x):
    return x * jax.nn.sigmoid(x)


def _ln(x):
    mu = jnp.mean(x, axis=-1, keepdims=True)
    xc = x - mu
    var = jnp.mean(xc * xc, axis=-1, keepdims=True)
    return xc * lax.rsqrt(var + EPS)


def _dot(a, b):
    return jnp.dot(a.astype(MXU_DT), b.astype(MXU_DT), preferred_element_type=F32)


def _dot_nt(a, b):
    return lax.dot_general(a.astype(MXU_DT), b.astype(MXU_DT), (((1,), (1,)), ((), ())),
                           preferred_element_type=F32)


def _mod_kernel(c_ref, w_ref, b_ref, o_ref):
    o_ref[...] = _dot(_silu(c_ref[...]), w_ref[...]) + b_ref[...]


def _modulation(c8, w_mod, b_mod, layer):
    d = c8.shape[1]
    n = w_mod.shape[2]
    tn = d // 2
    return pl.pallas_call(
        _mod_kernel,
        grid=(n // tn,),
        in_specs=[pl.BlockSpec((8, d), lambda j: (0, 0)),
                  pl.BlockSpec((None, d, tn), lambda j: (layer, 0, j)),
                  pl.BlockSpec((None, 1, tn), lambda j: (layer, 0, j))],
        out_specs=pl.BlockSpec((8, tn), lambda j: (0, j)),
        out_shape=jax.ShapeDtypeStruct((8, n), F32),
        compiler_params=_cp(("parallel",)),
        name="modulation",
    )(c8, w_mod, b_mod)


def _ln_mod_matmul_kernel(x_ref, mod_ref, w_ref, o_ref, h_ref, *, shift_row):
    @pl.when(pl.program_id(1) == 0)
    def _():
        h = _ln(x_ref[...]) * (1.0 + mod_ref[shift_row + 1:shift_row + 2, :]) + mod_ref[shift_row:shift_row + 1, :]
        h_ref[...] = h.astype(MXU_DT)

    o_ref[...] = jnp.dot(h_ref[...], w_ref[...], preferred_element_type=F32)


def _ln_mod_matmul(x, mod, w, layer, *, seg_of_tile, shift_row, tn):
    r, d = x.shape
    n = w.shape[2]
    return pl.pallas_call(
        functools.partial(_ln_mod_matmul_kernel, shift_row=shift_row),
        grid=(r // TM, n // tn),
        in_specs=[pl.BlockSpec((TM, d), lambda i, j: (i, 0)),
                  pl.BlockSpec((None, 6, d), lambda i, j: (seg_of_tile(i), 0, 0)),
                  pl.BlockSpec((None, d, tn), lambda i, j: (layer, 0, j))],
        out_specs=pl.BlockSpec((TM, tn), lambda i, j: (i, j)),
        out_shape=jax.ShapeDtypeStruct((r, n), F32),
        scratch_shapes=[pltpu.VMEM((TM, d), MXU_DT)],
        compiler_params=_cp(("parallel", "arbitrary")),
        name="in_proj",
    )(x, mod, w)


def _dn_conv_kernel(prev_ref, x_ref, next_ref, w_ref, o_ref, buf_ref, *, blocks_per_seq, n_lat_blocks):
    i = pl.program_id(0)
    part = pl.program_id(1)
    li = i % blocks_per_seq
    is_lat = i < n_lat_blocks
    prev_ok = jnp.logical_and(is_lat, li > 0)
    next_ok = jnp.logical_and(is_lat, li < blocks_per_seq - 1)
    buf_ref[0:8, :] = jnp.where(prev_ok, prev_ref[...], 0.0)
    buf_ref[8:8 + RB, :] = x_ref[...]
    buf_ref[8 + RB:16 + RB, :] = jnp.where(next_ok, next_ref[...], 0.0)
    pad = CONV_K // 2
    y = buf_ref[8 - pad:8 - pad + RB, :] * w_ref[0:1, :]
    for tap in range(1, CONV_K):
        y = y + buf_ref[8 - pad + tap:8 - pad + tap + RB, :] * w_ref[tap:tap + 1, :]
    y = _silu(y)
    normed = part < 2
    for h in range(DN_HEADS):
        yh = y[:, h * HEAD:(h + 1) * HEAD]
        inv = lax.rsqrt(jnp.sum(yh * yh, axis=-1, keepdims=True) + EPS)
        o_ref[:, h * HEAD:(h + 1) * HEAD] = yh * jnp.where(normed, inv, 1.0)


def _dn_conv(p, conv_w, layer, *, blocks_per_seq, n_lat_blocks):
    r = p.shape[0]
    nblk = r // RB
    sub = RB // 8
    last8 = r // 8 - 1
    return pl.pallas_call(
        functools.partial(_dn_conv_kernel, blocks_per_seq=blocks_per_seq, n_lat_blocks=n_lat_blocks),
        grid=(nblk, 3),
        in_specs=[pl.BlockSpec((8, DN_W), lambda i, c: (jnp.maximum(i * sub - 1, 0), c)),
                  pl.BlockSpec((RB, DN_W), lambda i, c: (i, c)),
                  pl.BlockSpec((8, DN_W), lambda i, c: (jnp.minimum((i + 1) * sub, last8), c)),
                  pl.BlockSpec((None, 8, DN_W), lambda i, c: (layer, 0, c))],
        out_specs=pl.BlockSpec((RB, DN_W), lambda i, c: (i, c)),
        out_shape=jax.ShapeDtypeStruct((r, 3 * DN_W), F32),
        scratch_shapes=[pltpu.VMEM((RB + 16, DN_W), F32)],
        compiler_params=_cp(("parallel", "parallel")),
        name="dn_conv",
    )(p, p, p, conv_w)


def _dn_scan_operands():
    return (((2 * CHUNK, HEAD), MXU_DT), ((CHUNK, HEAD), F32), ((HEAD, CHUNK), MXU_DT),
            ((CHUNK, CHUNK), MXU_DT), ((8, HEAD), F32))

def _col(x, lane, idx):
    return jnp.sum(jnp.where(lane == idx, x, 0.0), axis=-1, keepdims=True)


def _dn_prep_kernel(q_ref, k_ref, v_ref, ps_ref, par_ref,
                    wq_f, u_f, kg_f, qk_f, dl_f, wq_b, u_b, kg_b, qk_b, dl_b):
    h0 = pl.program_id(1) * DN_HPS
    nchunk = RB // CHUNK
    ps = ps_ref[...]
    lane = lax.broadcasted_iota(jnp.int32, (RB, HEAD), 1)
    rowc = lax.broadcasted_iota(jnp.int32, (RB, HEAD), 0) % CHUNK
    beta_all = jax.nn.sigmoid(ps)
    z = ps + par_ref[0:1, :]
    dt = jnp.maximum(z, 0.0) + jnp.log(1.0 + jnp.exp(-jnp.abs(z)))
    g_all = -jnp.exp(par_ref[1:2, :]) * dt
    gf = g_all
    gb = g_all
    s = 1
    while s < CHUNK:
        gf = gf + jnp.where(rowc >= s, pltpu.roll(gf, s, 0), 0.0)
        gb = gb + jnp.where(rowc < CHUNK - s, pltpu.roll(gb, RB - s, 0), 0.0)
        s *= 2
    tot_all = gf + gb - g_all

    r2 = lax.broadcasted_iota(jnp.int32, (RB, RB), 0)
    c2 = lax.broadcasted_iota(jnp.int32, (RB, RB), 1)
    eye = (r2 == c2).astype(F32)
    same = (r2 // CHUNK) == (c2 // CHUNK)
    incl = (jnp.logical_and(same, r2 >= c2), jnp.logical_and(same, r2 <= c2))
    level_masks = []
    bs = 2
    while bs < CHUNK:
        level_masks.append(jnp.logical_and((r2 // (2 * bs)) == (c2 // (2 * bs)), (r2 // bs) != (c2 // bs)))
        bs *= 2

    probs = [(hh, d) for hh in range(DN_HPS) for d in range(2)]
    qs = [q_ref[:, hh * HEAD:(hh + 1) * HEAD] * (HEAD ** -0.5) for hh in range(DN_HPS)]
    ks = [k_ref[:, hh * HEAD:(hh + 1) * HEAD] for hh in range(DN_HPS)]
    vs = [v_ref[:, hh * HEAD:(hh + 1) * HEAD] for hh in range(DN_HPS)]
    kk = [_dot_nt(k, k) for k in ks]
    qk = [_dot_nt(q, k) for q, k in zip(qs, ks)]
    bcol, gcol, tcol, dm, lm, t = [], [], [], [], [], []
    for hh, d in probs:
        col = d * DN_HEADS + h0 + hh
        bcol.append(_col(beta_all, lane, col))
        gcol.append(_col(gf if d == 0 else gb, lane, 2 * DN_HEADS + col))
        tcol.append(_col(tot_all, lane, 2 * DN_HEADS + col))
        gi = jnp.broadcast_to(gcol[-1], (RB, RB))
        dm.append(jnp.where(incl[d], jnp.exp(jnp.minimum(gi - gi.T, 0.0)), 0.0))
        lm.append(jnp.where(r2 != c2, bcol[-1] * kk[hh] * dm[-1], 0.0))
        t.append(eye - jnp.where((r2 // 2) == (c2 // 2), lm[-1], 0.0))
    for mask in level_masks:
        tc = [_dot(ti, jnp.where(mask, li, 0.0)) for ti, li in zip(t, lm)]
        t = [ti - _dot(tci, ti) for ti, tci in zip(t, tc)]
    stores = []
    outs = ((wq_f, u_f, kg_f, qk_f, dl_f), (wq_b, u_b, kg_b, qk_b, dl_b))
    for p, (hh, d) in enumerate(probs):
        eg = jnp.exp(gcol[p])
        sol_u = _dot(t[p], vs[hh] * bcol[p])
        sol_w = _dot(t[p], ks[hh] * (bcol[p] * eg))
        qkm = jnp.where(incl[d], qk[hh] * dm[p], 0.0)
        q_g = qs[hh] * eg
        k_gt = (ks[hh] * jnp.exp(tcol[p] - gcol[p])).T
        dl = jnp.exp(tcol[p])
        stores.append((outs[d], hh, d, sol_w, q_g, sol_u, k_gt, qkm, dl))
    for (wq_ref, u_ref, kg_ref, qk_ref, dl_ref), hh, d, sol_w, q_g, sol_u, k_gt, qkm, dl in stores:
        for j in range(nchunk):
            slot = j if d == 0 else nchunk - 1 - j
            rows = slice(j * CHUNK, (j + 1) * CHUNK)
            wq_ref[hh, slot, 0:CHUNK, :] = sol_w[rows, :].astype(wq_ref.dtype)
            wq_ref[hh, slot, CHUNK:2 * CHUNK, :] = q_g[rows, :].astype(wq_ref.dtype)
            u_ref[hh, slot, :, :] = sol_u[rows, :]
            kg_ref[hh, slot, :, :] = k_gt[:, rows].astype(kg_ref.dtype)
            qk_ref[hh, slot, :, :] = qkm[rows, rows].astype(qk_ref.dtype)
            dl_ref[hh, slot, :, :] = jnp.broadcast_to(dl[j * CHUNK:j * CHUNK + 1, :], (8, HEAD))


def _dn_prep(qkv, ps, par, *, batch, blocks_per_seq):
    r = qkv.shape[0]
    nblk = r // RB
    n_lat = batch * blocks_per_seq
    nchunk = RB // CHUNK
    npos = (blocks_per_seq + 1) * nchunk
    chains = batch * DN_HEADS

    def bidx(i):
        return jnp.where(i < n_lat, i // blocks_per_seq, i - n_lat)

    def pos_f(i):
        return jnp.where(i < n_lat, 1 + i % blocks_per_seq, 0)

    def pos_b(i):
        return jnp.where(i < n_lat, blocks_per_seq - i % blocks_per_seq, 0)

    hsteps = DN_HEADS // DN_HPS
    wide = DN_HPS * HEAD

    def ospec(shape, pos):
        return pl.BlockSpec((DN_HPS, nchunk) + shape, lambda i, h: (bidx(i) * hsteps + h, pos(i), 0, 0))

    shapes = [s for s, _ in _dn_scan_operands()]
    out_specs = [ospec(s, pos_f) for s in shapes] + [ospec(s, pos_b) for s in shapes]
    out_shape = [jax.ShapeDtypeStruct((chains, npos) + s, dt) for s, dt in _dn_scan_operands()] * 2
    return pl.pallas_call(
        _dn_prep_kernel,
        grid=(nblk, hsteps),
        in_specs=[pl.BlockSpec((RB, wide), lambda i, h: (i, h)),
                  pl.BlockSpec((RB, wide), lambda i, h: (i, hsteps + h)),
                  pl.BlockSpec((RB, wide), lambda i, h: (i, 2 * hsteps + h)),
                  pl.BlockSpec((RB, HEAD), lambda i, h: (i, 0)),
                  pl.BlockSpec((8, HEAD), lambda i, h: (0, 0))],
        out_specs=out_specs,
        out_shape=out_shape,
        compiler_params=_cp(("parallel", "parallel")),
        name="dn_prep",
    )(qkv, qkv, qkv, ps, par)


def _dn_scan_kernel(wq_f, u_f, kg_f, qk_f, dl_f, wq_b, u_b, kg_b, qk_b, dl_b, o_f, o_b, s_ref):
    @pl.when(pl.program_id(1) == 0)
    def _():
        s_ref[...] = jnp.zeros_like(s_ref)

    ins = ((wq_f, u_f, kg_f, qk_f, dl_f, o_f), (wq_b, u_b, kg_b, qk_b, dl_b, o_b))
    chains = [(d, c) for d in range(2) for c in range(DN_HEADS)]
    state = [s_ref[d * DN_HEADS + c] for d, c in chains]
    r = [_dot(ins[d][0][c, 0], s) for (d, c), s in zip(chains, state)]
    v_new = [ins[d][1][c, 0] - ri[0:CHUNK, :] for (d, c), ri in zip(chains, r)]
    out = [ri[CHUNK:2 * CHUNK, :] + _dot(ins[d][3][c, 0], vi) for (d, c), ri, vi in zip(chains, r, v_new)]
    new_state = [s * ins[d][4][c, 0, 0:1, :] + _dot(ins[d][2][c, 0], vi)
                 for (d, c), s, vi in zip(chains, state, v_new)]
    for (d, c), oi, si in zip(chains, out, new_state):
        ins[d][5][c, 0] = oi
        s_ref[d * DN_HEADS + c] = si


def _dn_scan(prep, *, batch):
    chains, npos = prep[0].shape[:2]
    shapes = [s for s, _ in _dn_scan_operands()]
    spec = lambda s: pl.BlockSpec((DN_HEADS, 1) + s, lambda b, n: (b, n, 0, 0))
    in_specs = [spec(s) for s in shapes] * 2
    o_sds = jax.ShapeDtypeStruct((chains, npos, CHUNK, HEAD), F32)
    return pl.pallas_call(
        _dn_scan_kernel,
        grid=(batch, npos),
        in_specs=in_specs,
        out_specs=[spec((CHUNK, HEAD)), spec((CHUNK, HEAD))],
        out_shape=[o_sds, o_sds],
        scratch_shapes=[pltpu.VMEM((2 * DN_HEADS, HEAD, HEAD), F32)],
        compiler_params=_cp(("parallel", "arbitrary")),
        name="dn_scan",
    )(*prep)


def _dn_out_kernel(of_ref, ob_ref, gate_ref, nw_ref, o_ref):
    nchunk = RB // CHUNK
    nw = nw_ref[0:1, :]
    for h in range(DN_HEADS):
        for j in range(nchunk):
            o = of_ref[h, j] + ob_ref[h, nchunk - 1 - j]
            y = o * lax.rsqrt(jnp.mean(o * o, axis=-1, keepdims=True) + EPS) * nw
            gt = gate_ref[j * CHUNK:(j + 1) * CHUNK, h * HEAD:(h + 1) * HEAD]
            o_ref[j * CHUNK:(j + 1) * CHUNK, h * HEAD:(h + 1) * HEAD] = (y * _silu(gt)).astype(o_ref.dtype)


def _dn_out(o_f, o_b, p, norm_w, *, batch, blocks_per_seq, nblk):
    n_lat = batch * blocks_per_seq
    nchunk = RB // CHUNK
    bidx = lambda i: jnp.where(i < n_lat, i // blocks_per_seq, i - n_lat)
    pos_f = lambda i: jnp.where(i < n_lat, 1 + i % blocks_per_seq, 0)
    pos_b = lambda i: jnp.where(i < n_lat, blocks_per_seq - i % blocks_per_seq, 0)
    return pl.pallas_call(
        _dn_out_kernel,
        grid=(nblk,),
        in_specs=[pl.BlockSpec((DN_HEADS, nchunk, CHUNK, HEAD), lambda i: (bidx(i), pos_f(i), 0, 0)),
                  pl.BlockSpec((DN_HEADS, nchunk, CHUNK, HEAD), lambda i: (bidx(i), pos_b(i), 0, 0)),
                  pl.BlockSpec((RB, DN_W), lambda i: (i, C_GATE // DN_W)),
                  pl.BlockSpec((8, HEAD), lambda i: (0, 0))],
        out_specs=pl.BlockSpec((RB, DN_W), lambda i: (i, 0)),
        out_shape=jax.ShapeDtypeStruct((nblk * RB, DN_W), MXU_DT),
        compiler_params=_cp(("parallel",)),
        name="dn_out",
    )(o_f, o_b, p, norm_w)


def _attn_prep_kernel(q_ref, k_ref, v_ref, cos_ref, sin_ref, qw_ref, kw_ref, qo_ref, ko_ref, vo_ref, *, n_lat_blocks):
    is_lat = pl.program_id(0) < n_lat_blocks
    cos = jnp.where(is_lat, cos_ref[...], 1.0)
    sin = jnp.where(is_lat, sin_ref[...], 0.0)
    lane = lax.broadcasted_iota(jnp.int32, (RB, HEAD), 1)
    first_half = (lane % (HEAD // 2)) < (HEAD // 4)

    def norm_rope(x, w):
        y = x * lax.rsqrt(jnp.mean(x * x, axis=-1, keepdims=True) + EPS) * w
        swapped = jnp.where(first_half, pltpu.roll(y, HEAD - HEAD // 4, 1), pltpu.roll(y, HEAD // 4, 1))
        return y * cos + swapped * sin

    q_scale = HEAD ** -0.5 * math.log2(math.e)
    ones_col = jnp.where(lane == 0, 1.0, 0.0).astype(vo_ref.dtype)
    for h in range(AQ_HEADS):
        qh = norm_rope(q_ref[:, h * HEAD:(h + 1) * HEAD], qw_ref[0:1, :]) * q_scale
        qo_ref[0, h] = qh.astype(qo_ref.dtype)
    for h in range(AKV_HEADS):
        ko_ref[0, h] = norm_rope(k_ref[:, h * HEAD:(h + 1) * HEAD], kw_ref[0:1, :]).astype(ko_ref.dtype)
        vo_ref[0, h, :, 0:HEAD] = v_ref[:, h * HEAD:(h + 1) * HEAD].astype(vo_ref.dtype)
        vo_ref[0, h, :, HEAD:2 * HEAD] = ones_col


def _attn_prep(p, cos, sin, qw, kw, *, batch, blocks_per_seq):
    r = p.shape[0]
    nblk = r // RB
    n_lat = batch * blocks_per_seq
    ltot = (blocks_per_seq + 1) * RB
    bidx = lambda i: jnp.where(i < n_lat, i // blocks_per_seq, i - n_lat)
    lblk = lambda i: jnp.where(i < n_lat, i % blocks_per_seq, blocks_per_seq)
    tblk = lambda i: jnp.where(i < n_lat, i % blocks_per_seq, 0)
    return pl.pallas_call(
        functools.partial(_attn_prep_kernel, n_lat_blocks=n_lat),
        grid=(nblk,),
        in_specs=[pl.BlockSpec((RB, AQ_W), lambda i: (i, C_AQ // AQ_W)),
                  pl.BlockSpec((RB, AKV_W), lambda i: (i, C_AK // AKV_W)),
                  pl.BlockSpec((RB, AKV_W), lambda i: (i, C_AV // AKV_W)),
                  pl.BlockSpec((RB, HEAD), lambda i: (tblk(i), 0)),
                  pl.BlockSpec((RB, HEAD), lambda i: (tblk(i), 0)),
                  pl.BlockSpec((8, HEAD), lambda i: (0, 0)),
                  pl.BlockSpec((8, HEAD), lambda i: (0, 0))],
        out_specs=[pl.BlockSpec((1, AQ_HEADS, RB, HEAD), lambda i: (bidx(i), 0, lblk(i), 0)),
                   pl.BlockSpec((1, AKV_HEADS, RB, HEAD), lambda i: (bidx(i), 0, lblk(i), 0)),
                   pl.BlockSpec((1, AKV_HEADS, RB, 2 * HEAD), lambda i: (bidx(i), 0, lblk(i), 0))],
        out_shape=[jax.ShapeDtypeStruct((batch, AQ_HEADS, ltot, HEAD), MXU_DT),
                   jax.ShapeDtypeStruct((batch, AKV_HEADS, ltot, HEAD), MXU_DT),
                   jax.ShapeDtypeStruct((batch, AKV_HEADS, ltot, 2 * HEAD), MXU_DT)],
        compiler_params=_cp(("parallel",)),
        name="attn_prep",
    )(p, p, p, cos, sin, qw, kw)


def _key_chunk(ltot, cap):
    return max(kc for kc in range(HEAD, cap + 1, HEAD) if ltot % kc == 0)


def _attn_kernel(q_ref, k_ref, v_ref, o_ref, *, n_lat_qblocks, lat_len, key_chunk):
    grp = AQ_HEADS // AKV_HEADS

    def attend(start, stop, kc):
        q = q_ref[0].reshape(grp * TQ, HEAD)
        m = None
        acc = None
        for c0 in range(start, stop, kc):
            s = lax.dot_general(q, k_ref[0, 0, c0:c0 + kc, :], (((1,), (1,)), ((), ())), preferred_element_type=F32)
            m_c = jnp.max(s, axis=-1, keepdims=True)
            m_new = m_c if m is None else jnp.maximum(m, m_c)
            pv = jnp.dot(jnp.exp2(s - m_new).astype(MXU_DT), v_ref[0, 0, c0:c0 + kc, :], preferred_element_type=F32)
            acc = pv if m is None else acc * jnp.exp2(m - m_new) + pv
            m = m_new
        o = acc[:, 0:HEAD] / acc[:, HEAD:HEAD + 1]
        for g in range(grp):
            o_ref[:, g * HEAD:(g + 1) * HEAD] = o[g * TQ:(g + 1) * TQ, :].astype(o_ref.dtype)

    qi = pl.program_id(2)
    ltot = k_ref.shape[2]

    @pl.when(qi < n_lat_qblocks)
    def _():
        attend(0, ltot, key_chunk)

    @pl.when(qi >= n_lat_qblocks)
    def _():
        attend(lat_len, ltot, ltot - lat_len)


def _attention(qh, kh, vh, *, lat_len, with_ctx, n_rows):
    batch, _, ltot, _ = qh.shape
    grp = AQ_HEADS // AKV_HEADS
    n_lat_q = lat_len // TQ
    n_q = ltot // TQ if with_ctx else n_lat_q
    n_ctx_q = (ltot - lat_len) // TQ

    def orow(b, qi):
        return jnp.where(qi < n_lat_q, b * n_lat_q + qi, batch * n_lat_q + b * n_ctx_q + (qi - n_lat_q))

    return pl.pallas_call(
        functools.partial(_attn_kernel, n_lat_qblocks=n_lat_q, lat_len=lat_len,
                          key_chunk=_key_chunk(ltot, ATTN_KEY_CHUNK_CAP)),
        grid=(batch, AKV_HEADS, n_q),
        in_specs=[pl.BlockSpec((1, grp, TQ, HEAD), lambda b, g, qi: (b, g, qi, 0)),
                  pl.BlockSpec((1, 1, ltot, HEAD), lambda b, g, qi: (b, g, 0, 0)),
                  pl.BlockSpec((1, 1, ltot, 2 * HEAD), lambda b, g, qi: (b, g, 0, 0))],
        out_specs=pl.BlockSpec((TQ, grp * HEAD), lambda b, g, qi: (orow(b, qi), g)),
        out_shape=jax.ShapeDtypeStruct((n_rows, AQ_W), MXU_DT),
        compiler_params=_cp(("parallel", "parallel", "arbitrary")),
        name="attention",
    )(qh, kh, vh)


def _plain_matmul_kernel(x_ref, w_ref, o_ref):
    o_ref[...] = _dot(x_ref[...], w_ref[...]).astype(o_ref.dtype)


def _fnet_channel_dft(p, cs):
    r = p.shape[0]
    return pl.pallas_call(
        _plain_matmul_kernel,
        grid=(r // TM,),
        in_specs=[pl.BlockSpec((TM, FN_W), lambda i: (i, C_FN // FN_W)),
                  pl.BlockSpec((FN_W, 2 * FN_W), lambda i: (0, 0))],
        out_specs=pl.BlockSpec((TM, 2 * FN_W), lambda i: (i, 0)),
        out_shape=jax.ShapeDtypeStruct((r, 2 * FN_W), MXU_DT),
        compiler_params=_cp(("parallel",)),
        name="fnet_channel_dft",
    )(p, cs)


def _fnet_pos_kernel(*refs, batch):
    cos_ref, sin_ref = refs[0], refs[1]
    y_refs = refs[2:2 + batch]
    w_ref = refs[2 + batch]
    o_ref = refs[3 + batch]
    acc_ref = refs[4 + batch]
    kk = pl.program_id(1)

    @pl.when(kk == 0)
    def _():
        acc_ref[...] = jnp.zeros_like(acc_ref)

    c = cos_ref[...]
    s = sin_ref[...]
    for b in range(batch):
        y = y_refs[b][...]
        acc_ref[b] += (jnp.dot(c, y[:, 0:FN_W], preferred_element_type=F32)
                       - jnp.dot(s, y[:, FN_W:2 * FN_W], preferred_element_type=F32))

    @pl.when(kk == pl.num_programs(1) - 1)
    def _():
        for b in range(batch):
            o_ref[b] = _dot(acc_ref[b], w_ref[...]).astype(o_ref.dtype)


def _fnet_pos_dft(y, cos_t, sin_t, fnet_w, layer, *, batch, seq_len, row_off):
    t = min(1024, seq_len)
    nt = seq_len // t
    y_specs = [pl.BlockSpec((t, 2 * FN_W), functools.partial(lambda i, k, b: (row_off // t + b * nt + k, 0), b=b))
               for b in range(batch)]
    return pl.pallas_call(
        functools.partial(_fnet_pos_kernel, batch=batch),
        grid=(nt, nt),
        in_specs=[pl.BlockSpec((t, t), lambda i, k: (i, k)),
                  pl.BlockSpec((t, t), lambda i, k: (i, k))] + y_specs +
                 [pl.BlockSpec((None, FN_W, FN_W), lambda i, k: (layer, 0, 0))],
        out_specs=pl.BlockSpec((batch, t, FN_W), lambda i, k: (0, i, 0)),
        out_shape=jax.ShapeDtypeStruct((batch, seq_len, FN_W), MXU_DT),
        scratch_shapes=[pltpu.VMEM((batch, t, FN_W), F32)],
        compiler_params=_cp(("parallel", "arbitrary")),
        name="fnet_pos_dft",
    )(cos_t, sin_t, *([y] * batch), fnet_w)


def _dft_tables(n):
    idx = jnp.arange(n, dtype=jnp.int32)
    ang = ((idx[:, None] * idx[None, :]) % n).astype(F32) * (2.0 * math.pi / n)
    scale = n ** -0.5
    return jnp.cos(ang) * scale, jnp.sin(ang) * scale


def _fft_stage1_kernel(y_ref, c_ref, s_ref, twc_ref, tws_ref, o_ref, *, j1_per_step):
    lane = lax.broadcasted_iota(jnp.int32, (HEAD, HEAD), 1)
    c = c_ref[...]
    s = s_ref[...]
    for t in range(j1_per_step):
        j1 = pl.program_id(1) * j1_per_step + t
        yc = y_ref[:, t * 2 * FN_W:t * 2 * FN_W + FN_W]
        ys = y_ref[:, t * 2 * FN_W + FN_W:(t + 1) * 2 * FN_W]
        ar = jnp.dot(c, yc, preferred_element_type=F32) - jnp.dot(s, ys, preferred_element_type=F32)
        ai = -(jnp.dot(s, yc, preferred_element_type=F32) + jnp.dot(c, ys, preferred_element_type=F32))
        tc = _col(twc_ref[...], lane, j1)
        ts = _col(tws_ref[...], lane, j1)
        o_ref[t, :, 0:FN_W] = (ar * tc + ai * ts).astype(o_ref.dtype)
        o_ref[t, :, FN_W:2 * FN_W] = (ai * tc - ar * ts).astype(o_ref.dtype)


def _fft_stage2_kernel(b_ref, c_ref, s_ref, w_ref, o_ref, *, k2_per_step):
    c = c_ref[...]
    s = s_ref[...]
    for t in range(k2_per_step):
        br = b_ref[:, t * 2 * FN_W:t * 2 * FN_W + FN_W]
        bi = b_ref[:, t * 2 * FN_W + FN_W:(t + 1) * 2 * FN_W]
        xr = jnp.dot(c, br, preferred_element_type=F32) + jnp.dot(s, bi, preferred_element_type=F32)
        o_ref[:, t * FN_W:(t + 1) * FN_W] = _dot(xr, w_ref[...]).astype(o_ref.dtype)


def _fnet_pos_fft(y, tabs, fnet_w, layer, *, batch, seq_len):
    c2, s2, twc, tws, c1, s1 = tabs
    l1 = seq_len // HEAD
    per = min(8, l1)
    yv = y.reshape(y.shape[0] // l1, l1 * 2 * FN_W)
    b_arr = pl.pallas_call(
        functools.partial(_fft_stage1_kernel, j1_per_step=per),
        grid=(batch, l1 // per),
        in_specs=[pl.BlockSpec((HEAD, per * 2 * FN_W), lambda b, j: (b, j)),
                  pl.BlockSpec((HEAD, HEAD), lambda b, j: (0, 0)),
                  pl.BlockSpec((HEAD, HEAD), lambda b, j: (0, 0)),
                  pl.BlockSpec((HEAD, HEAD), lambda b, j: (0, 0)),
                  pl.BlockSpec((HEAD, HEAD), lambda b, j: (0, 0))],
        out_specs=pl.BlockSpec((per, HEAD, 2 * FN_W), lambda b, j: (b * (l1 // per) + j, 0, 0)),
        out_shape=jax.ShapeDtypeStruct((batch * l1, HEAD, 2 * FN_W), MXU_DT),
        compiler_params=_cp(("parallel", "parallel")),
        name="fnet_fft_stage1",
    )(yv, c2, s2, twc, tws)
    bv = b_arr.reshape(batch, l1, HEAD * 2 * FN_W)
    out = pl.pallas_call(
        functools.partial(_fft_stage2_kernel, k2_per_step=8),
        grid=(batch, HEAD // 8),
        in_specs=[pl.BlockSpec((None, l1, 8 * 2 * FN_W), lambda b, k: (b, 0, k)),
                  pl.BlockSpec((l1, l1), lambda b, k: (0, 0)),
                  pl.BlockSpec((l1, l1), lambda b, k: (0, 0)),
                  pl.BlockSpec((None, FN_W, FN_W), lambda b, k: (layer, 0, 0))],
        out_specs=pl.BlockSpec((None, l1, 8 * FN_W), lambda b, k: (b, 0, k)),
        out_shape=jax.ShapeDtypeStruct((batch, l1, HEAD * FN_W), MXU_DT),
        compiler_params=_cp(("parallel", "parallel")),
        name="fnet_fft_stage2",
    )(bv, c1, s1, fnet_w)
    return out.reshape(batch * seq_len, FN_W)


def _fft_tables(seq_len):
    l1 = seq_len // HEAD
    c2, s2 = _dft_tables(HEAD)
    c1, s1 = _dft_tables(l1)
    k2 = jnp.arange(HEAD, dtype=jnp.int32)[:, None]
    j1 = jnp.arange(HEAD, dtype=jnp.int32)[None, :]
    ang = ((k2 * j1) % seq_len).astype(F32) * (2.0 * math.pi / seq_len)
    return (c2.astype(MXU_DT), s2.astype(MXU_DT), jnp.cos(ang), jnp.sin(ang), c1.astype(MXU_DT), s1.astype(MXU_DT))


def _out_proj_kernel(x_ref, dn_ref, at_ref, fn_ref, w_ref, mod_ref, g_ref, b_ref, o_ref, *, alpha):
    acc = jnp.dot(dn_ref[...], w_ref[0:DN_W, :], preferred_element_type=F32)
    acc += jnp.dot(at_ref[...], w_ref[DN_W:DN_W + AQ_W, :], preferred_element_type=F32)
    acc += jnp.dot(fn_ref[...], w_ref[DN_W + AQ_W:, :], preferred_element_type=F32)
    y = alpha * x_ref[...] + mod_ref[2:3, :] * acc
    o_ref[...] = _ln(y) * g_ref[0:1, :] + b_ref[0:1, :]


def _out_proj(x, a_dn, a_at, a_fn, w_out, mod, g, b, layer, *, seg_of_tile, n_rows, alpha):
    d = x.shape[1]
    tm = RB
    seg = lambda i: seg_of_tile(i // (TM // tm))
    return pl.pallas_call(
        functools.partial(_out_proj_kernel, alpha=alpha),
        grid=(n_rows // tm,),
        in_specs=[pl.BlockSpec((tm, d), lambda i: (i, 0)),
                  pl.BlockSpec((tm, DN_W), lambda i: (i, 0)),
                  pl.BlockSpec((tm, AQ_W), lambda i: (i, 0)),
                  pl.BlockSpec((tm, FN_W), lambda i: (i, 0)),
                  pl.BlockSpec((None, DN_W + AQ_W + FN_W, d), lambda i: (layer, 0, 0)),
                  pl.BlockSpec((None, 6, d), lambda i: (seg(i), 0, 0)),
                  pl.BlockSpec((None, 1, d), lambda i: (layer, 0, 0)),
                  pl.BlockSpec((None, 1, d), lambda i: (layer, 0, 0))],
        out_specs=pl.BlockSpec((tm, d), lambda i: (i, 0)),
        out_shape=jax.ShapeDtypeStruct((n_rows, d), F32),
        compiler_params=_cp(("parallel",)),
        name="out_proj",
    )(x, a_dn, a_at, a_fn, w_out, mod, g, b)


def _ffn_kernel(x_ref, mod_ref, wg_ref, wu_ref, wd_ref, g_ref, b_ref, o_ref, h_ref, acc_ref, *, alpha):
    f = pl.program_id(1)

    @pl.when(f == 0)
    def _():
        h = _ln(x_ref[...]) * (1.0 + mod_ref[4:5, :]) + mod_ref[3:4, :]
        h_ref[...] = h.astype(MXU_DT)
        acc_ref[...] = jnp.zeros_like(acc_ref)

    h = h_ref[...]
    gate = jnp.dot(h, wg_ref[...], preferred_element_type=F32)
    up = jnp.dot(h, wu_ref[...], preferred_element_type=F32)
    acc_ref[...] += jnp.dot((_silu(gate) * up).astype(MXU_DT), wd_ref[...], preferred_element_type=F32)

    @pl.when(f == pl.num_programs(1) - 1)
    def _():
        y = alpha * x_ref[...] + mod_ref[5:6, :] * acc_ref[...]
        o_ref[...] = _ln(y) * g_ref[0:1, :] + b_ref[0:1, :]


def _ffn(x, mod, wg, wu, wd, g, b, layer, li, *, seg_of_tile, n_rows, alpha, tf):
    d = x.shape[1]
    ff = wg.shape[2]
    return pl.pallas_call(
        functools.partial(_ffn_kernel, alpha=alpha),
        grid=(n_rows // TM, ff // tf),
        in_specs=[pl.BlockSpec((TM, d), lambda i, f: (i, 0)),
                  pl.BlockSpec((None, 6, d), lambda i, f: (seg_of_tile(i), 0, 0)),
                  pl.BlockSpec((None, d, tf), lambda i, f: (li, 0, f)),
                  pl.BlockSpec((None, d, tf), lambda i, f: (li, 0, f)),
                  pl.BlockSpec((None, tf, d), lambda i, f: (li, f, 0)),
                  pl.BlockSpec((None, 1, d), lambda i, f: (layer, 0, 0)),
                  pl.BlockSpec((None, 1, d), lambda i, f: (layer, 0, 0))],
        out_specs=pl.BlockSpec((TM, d), lambda i, f: (i, 0)),
        out_shape=jax.ShapeDtypeStruct((n_rows, d), F32),
        scratch_shapes=[pltpu.VMEM((TM, d), MXU_DT), pltpu.VMEM((TM, d), F32)],
        compiler_params=_cp(("parallel", "arbitrary")),
        name="ffn",
    )(x, mod, wg, wu, wd, g, b)


def _router_kernel(x_ref, mod_ref, rw_ref, h_ref, idx_ref, gate_ref):
    h = _ln(x_ref[...]) * (1.0 + mod_ref[4:5, :]) + mod_ref[3:4, :]
    h_ref[...] = h.astype(h_ref.dtype)
    logits = jnp.dot(h, rw_ref[...], precision=HI, preferred_element_type=F32)
    lane = lax.broadcasted_iota(jnp.int32, logits.shape, 1)
    neg = jnp.float32(-jnp.inf)
    l1 = jnp.where(lane < N_EXPERTS, logits, neg)
    m1 = jnp.max(l1, axis=-1, keepdims=True)
    i1 = jnp.min(jnp.where(l1 == m1, lane, HEAD), axis=-1, keepdims=True)
    l2 = jnp.where(lane == i1, neg, l1)
    m2 = jnp.max(l2, axis=-1, keepdims=True)
    i2 = jnp.min(jnp.where(l2 == m2, lane, HEAD), axis=-1, keepdims=True)
    e = jnp.exp(m2 - m1)
    g1 = 1.0 / (1.0 + e)
    g2 = e / (1.0 + e)
    idx_ref[...] = jnp.where(lane == 0, i1, jnp.where(lane == 1, i2, 0))
    gate_ref[...] = jnp.where(lane == 0, g1, jnp.where(lane == 1, g2, 0.0))


def _router(x, mod, rw, *, seg_of_tile, n_rows):
    d = x.shape[1]
    return pl.pallas_call(
        _router_kernel,
        grid=(n_rows // TM,),
        in_specs=[pl.BlockSpec((TM, d), lambda i: (i, 0)),
                  pl.BlockSpec((None, 6, d), lambda i: (seg_of_tile(i), 0, 0)),
                  pl.BlockSpec((d, HEAD), lambda i: (0, 0))],
        out_specs=[pl.BlockSpec((TM, d), lambda i: (i, 0)),
                   pl.BlockSpec((TM, HEAD), lambda i: (i, 0)),
                   pl.BlockSpec((TM, HEAD), lambda i: (i, 0))],
        out_shape=[jax.ShapeDtypeStruct((n_rows, d), MXU_DT),
                   jax.ShapeDtypeStruct((n_rows, HEAD), jnp.int32),
                   jax.ShapeDtypeStruct((n_rows, HEAD), F32)],
        compiler_params=_cp(("parallel",)),
        name="router",
    )(x, mod, rw)


def _moe_kernel(te_ref, nv_ref, h_ref, wg_ref, wu_ref, wd_ref, o_ref, acc_ref):
    t = pl.program_id(0)
    f = pl.program_id(1)
    last = pl.num_programs(1) - 1
    valid = t < nv_ref[0]

    @pl.when(jnp.logical_and(valid, f == 0))
    def _():
        acc_ref[...] = jnp.zeros_like(acc_ref)

    @pl.when(valid)
    def _():
        h = h_ref[...]
        gate = _dot(h, wg_ref[...])
        up = _dot(h, wu_ref[...])
        acc_ref[...] += _dot(_silu(gate) * up, wd_ref[...])

    @pl.when(jnp.logical_and(valid, f == last))
    def _():
        o_ref[...] = acc_ref[...]

    @pl.when(jnp.logical_and(jnp.logical_not(valid), f == last))
    def _():
        o_ref[...] = jnp.zeros_like(o_ref)


def _moe_experts(tile_expert, n_valid, h_sorted, wg, wu, wd, mi, *, tf):
    mp, d = h_sorted.shape
    ff = wg.shape[3]
    nf = ff // tf

    def fidx(t, f, nv):
        return jnp.where(t < nv[0], f, nf - 1)

    grid_spec = pltpu.PrefetchScalarGridSpec(
        num_scalar_prefetch=2,
        grid=(mp // TM_MOE, nf),
        in_specs=[pl.BlockSpec((TM_MOE, d), lambda t, f, te, nv: (t, 0)),
                  pl.BlockSpec((None, None, d, tf), lambda t, f, te, nv: (mi, te[t], 0, fidx(t, f, nv))),
                  pl.BlockSpec((None, None, d, tf), lambda t, f, te, nv: (mi, te[t], 0, fidx(t, f, nv))),
                  pl.BlockSpec((None, None, tf, d), lambda t, f, te, nv: (mi, te[t], fidx(t, f, nv), 0))],
        out_specs=pl.BlockSpec((TM_MOE, d), lambda t, f, te, nv: (t, 0)),
        scratch_shapes=[pltpu.VMEM((TM_MOE, d), F32)],
    )
    return pl.pallas_call(
        _moe_kernel,
        grid_spec=grid_spec,
        out_shape=jax.ShapeDtypeStruct((mp, d), F32),
        compiler_params=_cp(("arbitrary", "arbitrary")),
        name="moe_experts",
    )(tile_expert, n_valid, h_sorted, wg, wu, wd)


def _moe_combine_kernel(x_ref, y1_ref, y2_ref, gate_ref, mod_ref, g_ref, b_ref, o_ref, *, alpha):
    gates = gate_ref[...]
    y = gates[:, 0:1] * y1_ref[...] + gates[:, 1:2] * y2_ref[...]
    z = alpha * x_ref[...] + mod_ref[5:6, :] * y
    o_ref[...] = _ln(z) * g_ref[0:1, :] + b_ref[0:1, :]


def _moe_combine(x, y1, y2, gates, mod, g, b, layer, *, seg_of_tile, n_rows, alpha):
    d = x.shape[1]
    tm = RB
    seg = lambda i: seg_of_tile(i // (TM // tm))
    return pl.pallas_call(
        functools.partial(_moe_combine_kernel, alpha=alpha),
        grid=(n_rows // tm,),
        in_specs=[pl.BlockSpec((tm, d), lambda i: (i, 0)),
                  pl.BlockSpec((tm, d), lambda i: (i, 0)),
                  pl.BlockSpec((tm, d), lambda i: (i, 0)),
                  pl.BlockSpec((tm, HEAD), lambda i: (i, 0)),
                  pl.BlockSpec((None, 6, d), lambda i: (seg(i), 0, 0)),
                  pl.BlockSpec((None, 1, d), lambda i: (layer, 0, 0)),
                  pl.BlockSpec((None, 1, d), lambda i: (layer, 0, 0))],
        out_specs=pl.BlockSpec((tm, d), lambda i: (i, 0)),
        out_shape=jax.ShapeDtypeStruct((n_rows, d), F32),
        compiler_params=_cp(("parallel",)),
        name="moe_combine",
    )(x, y1, y2, gates, mod, g, b)


def _moe_layer(x, mod, rw, wg, wu, wd, g, b, layer, mi, *, seg_of_tile, n_rows, alpha, tf):
    h, idx, gates = _router(x, mod, rw, seg_of_tile=seg_of_tile, n_rows=n_rows)
    na = 2 * n_rows
    mp = na + N_EXPERTS * TM_MOE
    e_flat = idx[:, 0:2].reshape(na)
    onehot = (e_flat[:, None] == jnp.arange(N_EXPERTS, dtype=jnp.int32)[None, :]).astype(jnp.int32)
    csum = jnp.cumsum(onehot, axis=0)
    rank = jnp.sum((csum - onehot) * onehot, axis=1)
    counts = csum[-1]
    padded = ((counts + TM_MOE - 1) // TM_MOE) * TM_MOE
    ends = jnp.cumsum(padded)
    dest = (ends - padded)[e_flat] + rank
    src_token = jnp.zeros((mp,), jnp.int32).at[dest].set(jnp.arange(na, dtype=jnp.int32) // 2)
    tile_start = jnp.arange(mp // TM_MOE, dtype=jnp.int32) * TM_MOE
    tile_expert = jnp.sum((ends[None, :] <= tile_start[:, None]).astype(jnp.int32), axis=1)
    tile_expert = jnp.minimum(tile_expert, N_EXPERTS - 1)
    n_valid = (ends[-1] // TM_MOE).astype(jnp.int32).reshape(1)
    h_sorted = jnp.take(h, src_token, axis=0, mode="clip")
    y_sorted = _moe_experts(tile_expert, n_valid, h_sorted, wg, wu, wd, mi, tf=tf)
    dest2 = dest.reshape(n_rows, 2)
    y1 = jnp.take(y_sorted, dest2[:, 0], axis=0, mode="clip")
    y2 = jnp.take(y_sorted, dest2[:, 1], axis=0, mode="clip")
    return _moe_combine(x, y1, y2, gates, mod, g, b, layer, seg_of_tile=seg_of_tile, n_rows=n_rows, alpha=alpha)


def _rope_tables(seq_len):
    axis_dim = HEAD // 2
    inv = ROPE_THETA ** (-jnp.arange(0, axis_dim, 2, dtype=F32) / axis_dim)
    t = jnp.arange(seq_len, dtype=jnp.int32)
    row = (t // GRID_W).astype(F32)[:, None] * inv
    col = (t % GRID_W).astype(F32)[:, None] * inv
    cos = jnp.concatenate([jnp.cos(row), jnp.cos(row), jnp.cos(col), jnp.cos(col)], axis=-1)
    sin = jnp.concatenate([-jnp.sin(row), jnp.sin(row), -jnp.sin(col), jnp.sin(col)], axis=-1)
    return cos, sin


def _pad_rows8(v):
    return jnp.zeros((8, v.shape[-1]), F32).at[0].set(v.astype(F32))


def kernel(x, c, ctx, c_ctx, w_mod, b_mod, w_in, dn_conv, dn_a_log, dn_dt_bias, dn_norm, attn_q_norm, attn_k_norm,
           fnet_w, w_out, ln1_g, ln1_b, ln2_g, ln2_b, ffn_w_gate, ffn_w_up, ffn_w_down, router, moe_w_gate,
           moe_w_up, moe_w_down):
    batch, seq, d = x.shape
    ctx_len = ctx.shape[1]
    depth = w_mod.shape[0]
    ff = ffn_w_gate.shape[2]
    assert ctx_len == RB and seq % TM == 0 and (batch * ctx_len) % TM == 0 and TM % RB == 0
    alpha = (2 * depth) ** 0.25
    n_lat = batch * seq
    n_all = n_lat + batch * ctx_len
    tiles_per_seq = seq // TM
    blocks_per_seq = seq // RB
    seg_of_tile = lambda i: jnp.minimum(i // tiles_per_seq, batch)
    tf = 512 if ff % 512 == 0 else ff

    w_in_big = jnp.concatenate([w_in[:, :, :4 * DN_W], w_in[:, :, 4 * DN_W + 4 * DN_HEADS:]], axis=-1).astype(MXU_DT)
    w_in_small = jnp.pad(w_in[:, :, 4 * DN_W:4 * DN_W + 4 * DN_HEADS],
                         ((0, 0), (0, 0), (0, HEAD - 4 * DN_HEADS))).astype(MXU_DT)
    conv_w = jnp.pad(dn_conv, ((0, 0), (0, 8 - CONV_K), (0, 0)))
    w_out_c = w_out.astype(MXU_DT)
    fnet_w_c = fnet_w.astype(MXU_DT)
    ffn_wg, ffn_wu, ffn_wd = ffn_w_gate.astype(MXU_DT), ffn_w_up.astype(MXU_DT), ffn_w_down.astype(MXU_DT)
    router_p = jnp.pad(router, ((0, 0), (0, 0), (0, HEAD - N_EXPERTS)))
    b_mod3 = b_mod[:, None, :]
    ln1_g3, ln1_b3, ln2_g3, ln2_b3 = ln1_g[:, None, :], ln1_b[:, None, :], ln2_g[:, None, :], ln2_b[:, None, :]

    rope_cos, rope_sin = _rope_tables(seq)
    lat_tabs = _fft_tables(seq)
    ctx_cos, ctx_sin = (t.astype(MXU_DT) for t in _dft_tables(ctx_len))
    ch_cos, ch_sin = _dft_tables(HEAD)
    eye_g = jnp.eye(FN_GROUPS, dtype=F32)
    ch_cs = jnp.concatenate([jnp.kron(eye_g, ch_cos), jnp.kron(eye_g, ch_sin)], axis=-1).astype(MXU_DT)

    c8 = jnp.zeros((8, d), F32).at[0:batch].set(c).at[batch].set(c_ctx)
    xs = jnp.concatenate([x.reshape(n_lat, d), ctx.reshape(batch * ctx_len, d)], axis=0)

    for layer in range(depth):
        need_ctx = layer < depth - 1
        rows_out = n_all if need_ctx else n_lat
        mod = _modulation(c8, w_mod, b_mod3, layer).reshape(8, 6, d)

        p = _ln_mod_matmul(xs, mod, w_in_big, layer, seg_of_tile=seg_of_tile, shift_row=0, tn=P_COLS // 4)
        ps = _ln_mod_matmul(xs, mod, w_in_small, layer, seg_of_tile=seg_of_tile, shift_row=0, tn=HEAD)

        qkv = _dn_conv(p, conv_w, layer, blocks_per_seq=blocks_per_seq, n_lat_blocks=batch * blocks_per_seq)
        par = (jnp.zeros((8, HEAD), F32)
               .at[0, 2 * DN_HEADS:4 * DN_HEADS].set(dn_dt_bias[layer].reshape(-1))
               .at[1, 2 * DN_HEADS:4 * DN_HEADS].set(dn_a_log[layer].reshape(-1)))
        prep = _dn_prep(qkv, ps, par, batch=batch, blocks_per_seq=blocks_per_seq)
        o_f, o_b = _dn_scan(prep, batch=batch)
        a_dn = _dn_out(o_f, o_b, p, _pad_rows8(dn_norm[layer]), batch=batch, blocks_per_seq=blocks_per_seq,
                       nblk=rows_out // RB)

        qh, kh, vh = _attn_prep(p, rope_cos, rope_sin, _pad_rows8(attn_q_norm[layer]), _pad_rows8(attn_k_norm[layer]),
                                batch=batch, blocks_per_seq=blocks_per_seq)
        a_at = _attention(qh, kh, vh, lat_len=seq, with_ctx=need_ctx, n_rows=rows_out)

        y = _fnet_channel_dft(p, ch_cs)
        a_fn = _fnet_pos_fft(y, lat_tabs, fnet_w_c, layer, batch=batch, seq_len=seq)
        if need_ctx:
            a_fn_c = _fnet_pos_dft(y, ctx_cos, ctx_sin, fnet_w_c, layer, batch=batch, seq_len=ctx_len, row_off=n_lat)
            a_fn = jnp.concatenate([a_fn, a_fn_c.reshape(batch * ctx_len, FN_W)], axis=0)

        xs = _out_proj(xs, a_dn, a_at, a_fn, w_out_c, mod, ln1_g3, ln1_b3, layer, seg_of_tile=seg_of_tile,
                       n_rows=rows_out, alpha=alpha)

        li = layer // 2
        if layer % 2 == 0:
            xs = _ffn(xs, mod, ffn_wg, ffn_wu, ffn_wd, ln2_g3, ln2_b3, layer, li, seg_of_tile=seg_of_tile,
                      n_rows=rows_out, alpha=alpha, tf=tf)
        else:
            xs = _moe_layer(xs, mod, router_p[li], moe_w_gate, moe_w_up, moe_w_down, ln2_g3, ln2_b3, layer, li,
                            seg_of_tile=seg_of_tile, n_rows=rows_out, alpha=alpha,
                            tf=TF_MOE if ff % TF_MOE == 0 else ff)
    return xs[:n_lat].reshape(batch, seq, d)
```

```python
import functools
import math

import jax
import jax.numpy as jnp
from jax import lax
from jax.experimental import pallas as pl
from jax.experimental.pallas import tpu as pltpu

F32 = jnp.float32
MXU_DT = jnp.bfloat16
HI = lax.Precision.HIGHEST

EPS = 1e-6
HEAD = 128
DN_HEADS = 8
AQ_HEADS = 4
AKV_HEADS = 2
FN_GROUPS = 4
CHUNK = 64
CONV_K = 5
GRID_W = 64
ROPE_THETA = 10000.0
N_EXPERTS = 8
DN_W = DN_HEADS * HEAD
AQ_W = AQ_HEADS * HEAD
AKV_W = AKV_HEADS * HEAD
FN_W = FN_GROUPS * HEAD
C_DNQ, C_DNK, C_DNV = 0, DN_W, 2 * DN_W
C_GATE = 3 * DN_W
C_AQ = 4 * DN_W
C_AK = C_AQ + AQ_W
C_AV = C_AK + AKV_W
C_FN = C_AV + AKV_W
P_COLS = C_FN + FN_W

TM = 512
TM_MOE = 768
TF_MOE = 256
RB = 256
DN_HPS = 4
TQ = 256
ATTN_KEY_CHUNK_CAP = 1536
V7X_VMEM_LIMIT = 56 * 1024 * 1024


def _cp(sem, vmem=V7X_VMEM_LIMIT):
    return pltpu.CompilerParams(dimension_semantics=sem, vmem_limit_bytes=vmem)


def _silu(x):
    return x * jax.nn.sigmoid(x)


def _ln(x):
    mu = jnp.mean(x, axis=-1, keepdims=True)
    xc = x - mu
    var = jnp.mean(xc * xc, axis=-1, keepdims=True)
    return xc * lax.rsqrt(var + EPS)


def _dot(a, b):
    return jnp.dot(a.astype(MXU_DT), b.astype(MXU_DT), preferred_element_type=F32)


def _dot_nt(a, b):
    return lax.dot_general(a.astype(MXU_DT), b.astype(MXU_DT), (((1,), (1,)), ((), ())),
                           preferred_element_type=F32)


def _mod_kernel(c_ref, w_ref, b_ref, o_ref):
    o_ref[...] = _dot(_silu(c_ref[...]), w_ref[...]) + b_ref[...]


def _modulation(c8, w_mod, b_mod, layer):
    d = c8.shape[1]
    n = w_mod.shape[2]
    tn = d // 2
    return pl.pallas_call(
        _mod_kernel,
        grid=(n // tn,),
        in_specs=[pl.BlockSpec((8, d), lambda j: (0, 0)),
                  pl.BlockSpec((None, d, tn), lambda j: (layer, 0, j)),
                  pl.BlockSpec((None, 1, tn), lambda j: (layer, 0, j))],
        out_specs=pl.BlockSpec((8, tn), lambda j: (0, j)),
        out_shape=jax.ShapeDtypeStruct((8, n), F32),
        compiler_params=_cp(("parallel",)),
        name="modulation",
    )(c8, w_mod, b_mod)


def _ln_mod_matmul_kernel(x_ref, mod_ref, w_ref, o_ref, h_ref, *, shift_row):
    @pl.when(pl.program_id(1) == 0)
    def _():
        h = _ln(x_ref[...]) * (1.0 + mod_ref[shift_row + 1:shift_row + 2, :]) + mod_ref[shift_row:shift_row + 1, :]
        h_ref[...] = h.astype(MXU_DT)

    o_ref[...] = jnp.dot(h_ref[...], w_ref[...], preferred_element_type=F32)


def _ln_mod_matmul(x, mod, w, layer, *, seg_of_tile, shift_row, tn):
    r, d = x.shape
    n = w.shape[2]
    return pl.pallas_call(
        functools.partial(_ln_mod_matmul_kernel, shift_row=shift_row),
        grid=(r // TM, n // tn),
        in_specs=[pl.BlockSpec((TM, d), lambda i, j: (i, 0)),
                  pl.BlockSpec((None, 6, d), lambda i, j: (seg_of_tile(i), 0, 0)),
                  pl.BlockSpec((None, d, tn), lambda i, j: (layer, 0, j))],
        out_specs=pl.BlockSpec((TM, tn), lambda i, j: (i, j)),
        out_shape=jax.ShapeDtypeStruct((r, n), F32),
        scratch_shapes=[pltpu.VMEM((TM, d), MXU_DT)],
        compiler_params=_cp(("parallel", "arbitrary")),
        name="in_proj",
    )(x, mod, w)


def _dn_conv_kernel(prev_ref, x_ref, next_ref, w_ref, o_ref, buf_ref, *, blocks_per_seq, n_lat_blocks):
    i = pl.program_id(0)
    part = pl.program_id(1)
    li = i % blocks_per_seq
    is_lat = i < n_lat_blocks
    prev_ok = jnp.logical_and(is_lat, li > 0)
    next_ok = jnp.logical_and(is_lat, li < blocks_per_seq - 1)
    buf_ref[0:8, :] = jnp.where(prev_ok, prev_ref[...], 0.0)
    buf_ref[8:8 + RB, :] = x_ref[...]
    buf_ref[8 + RB:16 + RB, :] = jnp.where(next_ok, next_ref[...], 0.0)
    pad = CONV_K // 2
    y = buf_ref[8 - pad:8 - pad + RB, :] * w_ref[0:1, :]
    for tap in range(1, CONV_K):
        y = y + buf_ref[8 - pad + tap:8 - pad + tap + RB, :] * w_ref[tap:tap + 1, :]
    y = _silu(y)
    normed = part < 2
    for h in range(DN_HEADS):
        yh = y[:, h * HEAD:(h + 1) * HEAD]
        inv = lax.rsqrt(jnp.sum(yh * yh, axis=-1, keepdims=True) + EPS)
        o_ref[:, h * HEAD:(h + 1) * HEAD] = yh * jnp.where(normed, inv, 1.0)


def _dn_conv(p, conv_w, layer, *, blocks_per_seq, n_lat_blocks):
    r = p.shape[0]
    nblk = r // RB
    sub = RB // 8
    last8 = r // 8 - 1
    return pl.pallas_call(
        functools.partial(_dn_conv_kernel, blocks_per_seq=blocks_per_seq, n_lat_blocks=n_lat_blocks),
        grid=(nblk, 3),
        in_specs=[pl.BlockSpec((8, DN_W), lambda i, c: (jnp.maximum(i * sub - 1, 0), c)),
                  pl.BlockSpec((RB, DN_W), lambda i, c: (i, c)),
                  pl.BlockSpec((8, DN_W), lambda i, c: (jnp.minimum((i + 1) * sub, last8), c)),
                  pl.BlockSpec((None, 8, DN_W), lambda i, c: (layer, 0, c))],
        out_specs=pl.BlockSpec((RB, DN_W), lambda i, c: (i, c)),
        out_shape=jax.ShapeDtypeStruct((r, 3 * DN_W), F32),
        scratch_shapes=[pltpu.VMEM((RB + 16, DN_W), F32)],
        compiler_params=_cp(("parallel", "parallel")),
        name="dn_conv",
    )(p, p, p, conv_w)


def _dn_scan_operands():
    return (((2 * CHUNK, HEAD), MXU_DT), ((CHUNK, HEAD), F32), ((HEAD, CHUNK), MXU_DT),
            ((CHUNK, CHUNK), MXU_DT), ((8, HEAD), F32))

def _col(x, lane, idx):
    return jnp.sum(jnp.where(lane == idx, x, 0.0), axis=-1, keepdims=True)


def _dn_prep_kernel(q_ref, k_ref, v_ref, ps_ref, par_ref,
                    wq_f, u_f, kg_f, qk_f, dl_f, wq_b, u_b, kg_b, qk_b, dl_b):
    h0 = pl.program_id(1) * DN_HPS
    nchunk = RB // CHUNK
    ps = ps_ref[...]
    lane = lax.broadcasted_iota(jnp.int32, (RB, HEAD), 1)
    rowc = lax.broadcasted_iota(jnp.int32, (RB, HEAD), 0) % CHUNK
    beta_all = jax.nn.sigmoid(ps)
    z = ps + par_ref[0:1, :]
    dt = jnp.maximum(z, 0.0) + jnp.log(1.0 + jnp.exp(-jnp.abs(z)))
    g_all = -jnp.exp(par_ref[1:2, :]) * dt
    gf = g_all
    gb = g_all
    s = 1
    while s < CHUNK:
        gf = gf + jnp.where(rowc >= s, pltpu.roll(gf, s, 0), 0.0)
        gb = gb + jnp.where(rowc < CHUNK - s, pltpu.roll(gb, RB - s, 0), 0.0)
        s *= 2
    tot_all = gf + gb - g_all

    r2 = lax.broadcasted_iota(jnp.int32, (RB, RB), 0)
    c2 = lax.broadcasted_iota(jnp.int32, (RB, RB), 1)
    eye = (r2 == c2).astype(F32)
    same = (r2 // CHUNK) == (c2 // CHUNK)
    incl = (jnp.logical_and(same, r2 >= c2), jnp.logical_and(same, r2 <= c2))
    level_masks = []
    bs = 2
    while bs < CHUNK:
        level_masks.append(jnp.logical_and((r2 // (2 * bs)) == (c2 // (2 * bs)), (r2 // bs) != (c2 // bs)))
        bs *= 2

    probs = [(hh, d) for hh in range(DN_HPS) for d in range(2)]
    qs = [q_ref[:, hh * HEAD:(hh + 1) * HEAD] * (HEAD ** -0.5) for hh in range(DN_HPS)]
    ks = [k_ref[:, hh * HEAD:(hh + 1) * HEAD] for hh in range(DN_HPS)]
    vs = [v_ref[:, hh * HEAD:(hh + 1) * HEAD] for hh in range(DN_HPS)]
    kk = [_dot_nt(k, k) for k in ks]
    qk = [_dot_nt(q, k) for q, k in zip(qs, ks)]
    bcol, gcol, tcol, dm, lm, t = [], [], [], [], [], []
    for hh, d in probs:
        col = d * DN_HEADS + h0 + hh
        bcol.append(_col(beta_all, lane, col))
        gcol.append(_col(gf if d == 0 else gb, lane, 2 * DN_HEADS + col))
        tcol.append(_col(tot_all, lane, 2 * DN_HEADS + col))
        gi = jnp.broadcast_to(gcol[-1], (RB, RB))
        dm.append(jnp.where(incl[d], jnp.exp(jnp.minimum(gi - gi.T, 0.0)), 0.0))
        lm.append(jnp.where(r2 != c2, bcol[-1] * kk[hh] * dm[-1], 0.0))
        t.append(eye - jnp.where((r2 // 2) == (c2 // 2), lm[-1], 0.0))
    for mask in level_masks:
        t_op = [ti.astype(MXU_DT) for ti in t]
        tc = [_dot(ti, jnp.where(mask, li, 0.0)) for ti, li in zip(t_op, lm)]
        t = [_dot(eye - tci, ti) for ti, tci in zip(t_op, tc)]
    stores = []
    outs = ((wq_f, u_f, kg_f, qk_f, dl_f), (wq_b, u_b, kg_b, qk_b, dl_b))
    for p, (hh, d) in enumerate(probs):
        eg = jnp.exp(gcol[p])
        sol_u = _dot(t[p], vs[hh] * bcol[p])
        sol_w = _dot(t[p], ks[hh] * (bcol[p] * eg))
        qkm = jnp.where(incl[d], qk[hh] * dm[p], 0.0)
        q_g = qs[hh] * eg
        k_gt = (ks[hh] * jnp.exp(tcol[p] - gcol[p])).T
        dl = jnp.exp(tcol[p])
        stores.append((outs[d], hh, d, sol_w, q_g, sol_u, k_gt, qkm, dl))
    for (wq_ref, u_ref, kg_ref, qk_ref, dl_ref), hh, d, sol_w, q_g, sol_u, k_gt, qkm, dl in stores:
        for j in range(nchunk):
            slot = j if d == 0 else nchunk - 1 - j
            rows = slice(j * CHUNK, (j + 1) * CHUNK)
            wq_ref[hh, slot, 0:CHUNK, :] = sol_w[rows, :].astype(wq_ref.dtype)
            wq_ref[hh, slot, CHUNK:2 * CHUNK, :] = q_g[rows, :].astype(wq_ref.dtype)
            u_ref[hh, slot, :, :] = sol_u[rows, :]
            kg_ref[hh, slot, :, :] = k_gt[:, rows].astype(kg_ref.dtype)
            qk_ref[hh, slot, :, :] = qkm[rows, rows].astype(qk_ref.dtype)
            dl_ref[hh, slot, :, :] = jnp.broadcast_to(dl[j * CHUNK:j * CHUNK + 1, :], (8, HEAD))


def _dn_prep(qkv, ps, par, *, batch, blocks_per_seq):
    r = qkv.shape[0]
    nblk = r // RB
    n_lat = batch * blocks_per_seq
    nchunk = RB // CHUNK
    npos = (blocks_per_seq + 1) * nchunk
    chains = batch * DN_HEADS

    def bidx(i):
        return jnp.where(i < n_lat, i // blocks_per_seq, i - n_lat)

    def pos_f(i):
        return jnp.where(i < n_lat, 1 + i % blocks_per_seq, 0)

    def pos_b(i):
        return jnp.where(i < n_lat, blocks_per_seq - i % blocks_per_seq, 0)

    hsteps = DN_HEADS // DN_HPS
    wide = DN_HPS * HEAD

    def ospec(shape, pos):
        return pl.BlockSpec((DN_HPS, nchunk) + shape, lambda i, h: (bidx(i) * hsteps + h, pos(i), 0, 0))

    shapes = [s for s, _ in _dn_scan_operands()]
    out_specs = [ospec(s, pos_f) for s in shapes] + [ospec(s, pos_b) for s in shapes]
    out_shape = [jax.ShapeDtypeStruct((chains, npos) + s, dt) for s, dt in _dn_scan_operands()] * 2
    return pl.pallas_call(
        _dn_prep_kernel,
        grid=(nblk, hsteps),
        in_specs=[pl.BlockSpec((RB, wide), lambda i, h: (i, h)),
                  pl.BlockSpec((RB, wide), lambda i, h: (i, hsteps + h)),
                  pl.BlockSpec((RB, wide), lambda i, h: (i, 2 * hsteps + h)),
                  pl.BlockSpec((RB, HEAD), lambda i, h: (i, 0)),
                  pl.BlockSpec((8, HEAD), lambda i, h: (0, 0))],
        out_specs=out_specs,
        out_shape=out_shape,
        compiler_params=_cp(("parallel", "parallel")),
        name="dn_prep",
    )(qkv, qkv, qkv, ps, par)


def _dn_scan_kernel(wq_f, u_f, kg_f, qk_f, dl_f, wq_b, u_b, kg_b, qk_b, dl_b, o_f, o_b, s_ref):
    @pl.when(pl.program_id(0) == 0)
    def _():
        s_ref[...] = jnp.zeros_like(s_ref)

    ins = ((wq_f, u_f, kg_f, qk_f, dl_f, o_f), (wq_b, u_b, kg_b, qk_b, dl_b, o_b))
    nchain = wq_f.shape[0]
    chains = [(d, c) for d in range(2) for c in range(nchain)]
    state = [s_ref[d * nchain + c] for d, c in chains]
    r = [_dot(ins[d][0][c, 0], s) for (d, c), s in zip(chains, state)]
    v_new = [ins[d][1][c, 0] - ri[0:CHUNK, :] for (d, c), ri in zip(chains, r)]
    out = [ri[CHUNK:2 * CHUNK, :] + _dot(ins[d][3][c, 0], vi) for (d, c), ri, vi in zip(chains, r, v_new)]
    new_state = [s * ins[d][4][c, 0, 0:1, :] + _dot(ins[d][2][c, 0], vi)
                 for (d, c), s, vi in zip(chains, state, v_new)]
    for (d, c), oi, si in zip(chains, out, new_state):
        ins[d][5][c, 0] = oi
        s_ref[d * nchain + c] = si


def _dn_scan(prep):
    chains, npos = prep[0].shape[:2]
    shapes = [s for s, _ in _dn_scan_operands()]
    spec = lambda s: pl.BlockSpec((chains, 1) + s, lambda n: (0, n, 0, 0))
    in_specs = [spec(s) for s in shapes] * 2
    o_sds = jax.ShapeDtypeStruct((chains, npos, CHUNK, HEAD), F32)
    return pl.pallas_call(
        _dn_scan_kernel,
        grid=(npos,),
        in_specs=in_specs,
        out_specs=[spec((CHUNK, HEAD)), spec((CHUNK, HEAD))],
        out_shape=[o_sds, o_sds],
        scratch_shapes=[pltpu.VMEM((2 * chains, HEAD, HEAD), F32)],
        compiler_params=_cp(("arbitrary",)),
        name="dn_scan",
    )(*prep)


def _dn_out_kernel(of_ref, ob_ref, gate_ref, nw_ref, o_ref):
    nchunk = RB // CHUNK
    nw = nw_ref[0:1, :]
    for h in range(DN_HEADS):
        for j in range(nchunk):
            o = of_ref[h, j] + ob_ref[h, nchunk - 1 - j]
            y = o * lax.rsqrt(jnp.mean(o * o, axis=-1, keepdims=True) + EPS) * nw
            gt = gate_ref[j * CHUNK:(j + 1) * CHUNK, h * HEAD:(h + 1) * HEAD]
            o_ref[j * CHUNK:(j + 1) * CHUNK, h * HEAD:(h + 1) * HEAD] = (y * _silu(gt)).astype(o_ref.dtype)


def _dn_out(o_f, o_b, p, norm_w, *, batch, blocks_per_seq, nblk):
    n_lat = batch * blocks_per_seq
    nchunk = RB // CHUNK
    bidx = lambda i: jnp.where(i < n_lat, i // blocks_per_seq, i - n_lat)
    pos_f = lambda i: jnp.where(i < n_lat, 1 + i % blocks_per_seq, 0)
    pos_b = lambda i: jnp.where(i < n_lat, blocks_per_seq - i % blocks_per_seq, 0)
    return pl.pallas_call(
        _dn_out_kernel,
        grid=(nblk,),
        in_specs=[pl.BlockSpec((DN_HEADS, nchunk, CHUNK, HEAD), lambda i: (bidx(i), pos_f(i), 0, 0)),
                  pl.BlockSpec((DN_HEADS, nchunk, CHUNK, HEAD), lambda i: (bidx(i), pos_b(i), 0, 0)),
                  pl.BlockSpec((RB, DN_W), lambda i: (i, C_GATE // DN_W)),
                  pl.BlockSpec((8, HEAD), lambda i: (0, 0))],
        out_specs=pl.BlockSpec((RB, DN_W), lambda i: (i, 0)),
        out_shape=jax.ShapeDtypeStruct((nblk * RB, DN_W), MXU_DT),
        compiler_params=_cp(("parallel",)),
        name="dn_out",
    )(o_f, o_b, p, norm_w)


def _attn_prep_kernel(q_ref, k_ref, v_ref, cos_ref, sin_ref, qw_ref, kw_ref, qo_ref, ko_ref, vo_ref, *, n_lat_blocks):
    is_lat = pl.program_id(0) < n_lat_blocks
    cos = jnp.where(is_lat, cos_ref[...], 1.0)
    sin = jnp.where(is_lat, sin_ref[...], 0.0)
    lane = lax.broadcasted_iota(jnp.int32, (RB, HEAD), 1)
    first_half = (lane % (HEAD // 2)) < (HEAD // 4)

    def norm_rope(x, w):
        y = x * lax.rsqrt(jnp.mean(x * x, axis=-1, keepdims=True) + EPS) * w
        swapped = jnp.where(first_half, pltpu.roll(y, HEAD - HEAD // 4, 1), pltpu.roll(y, HEAD // 4, 1))
        return y * cos + swapped * sin

    q_scale = HEAD ** -0.5 * math.log2(math.e)
    ones_col = jnp.where(lane == 0, 1.0, 0.0).astype(vo_ref.dtype)
    for h in range(AQ_HEADS):
        qh = norm_rope(q_ref[:, h * HEAD:(h + 1) * HEAD], qw_ref[0:1, :]) * q_scale
        qo_ref[0, h] = qh.astype(qo_ref.dtype)
    for h in range(AKV_HEADS):
        ko_ref[0, h] = norm_rope(k_ref[:, h * HEAD:(h + 1) * HEAD], kw_ref[0:1, :]).astype(ko_ref.dtype)
        vo_ref[0, h, :, 0:HEAD] = v_ref[:, h * HEAD:(h + 1) * HEAD].astype(vo_ref.dtype)
        vo_ref[0, h, :, HEAD:2 * HEAD] = ones_col


def _attn_prep(p, cos, sin, qw, kw, *, batch, blocks_per_seq):
    r = p.shape[0]
    nblk = r // RB
    n_lat = batch * blocks_per_seq
    ltot = (blocks_per_seq + 1) * RB
    bidx = lambda i: jnp.where(i < n_lat, i // blocks_per_seq, i - n_lat)
    lblk = lambda i: jnp.where(i < n_lat, i % blocks_per_seq, blocks_per_seq)
    tblk = lambda i: jnp.where(i < n_lat, i % blocks_per_seq, 0)
    return pl.pallas_call(
        functools.partial(_attn_prep_kernel, n_lat_blocks=n_lat),
        grid=(nblk,),
        in_specs=[pl.BlockSpec((RB, AQ_W), lambda i: (i, C_AQ // AQ_W)),
                  pl.BlockSpec((RB, AKV_W), lambda i: (i, C_AK // AKV_W)),
                  pl.BlockSpec((RB, AKV_W), lambda i: (i, C_AV // AKV_W)),
                  pl.BlockSpec((RB, HEAD), lambda i: (tblk(i), 0)),
                  pl.BlockSpec((RB, HEAD), lambda i: (tblk(i), 0)),
                  pl.BlockSpec((8, HEAD), lambda i: (0, 0)),
                  pl.BlockSpec((8, HEAD), lambda i: (0, 0))],
        out_specs=[pl.BlockSpec((1, AQ_HEADS, RB, HEAD), lambda i: (bidx(i), 0, lblk(i), 0)),
                   pl.BlockSpec((1, AKV_HEADS, RB, HEAD), lambda i: (bidx(i), 0, lblk(i), 0)),
                   pl.BlockSpec((1, AKV_HEADS, RB, 2 * HEAD), lambda i: (bidx(i), 0, lblk(i), 0))],
        out_shape=[jax.ShapeDtypeStruct((batch, AQ_HEADS, ltot, HEAD), MXU_DT),
                   jax.ShapeDtypeStruct((batch, AKV_HEADS, ltot, HEAD), MXU_DT),
                   jax.ShapeDtypeStruct((batch, AKV_HEADS, ltot, 2 * HEAD), MXU_DT)],
        compiler_params=_cp(("parallel",)),
        name="attn_prep",
    )(p, p, p, cos, sin, qw, kw)


def _key_chunk(ltot, cap):
    return max(kc for kc in range(HEAD, cap + 1, HEAD) if ltot % kc == 0)


def _attn_kernel(q_ref, k_ref, v_ref, o_ref, *, n_lat_qblocks, lat_len, key_chunk):
    grp = AQ_HEADS // AKV_HEADS

    def attend(start, stop, kc):
        q = q_ref[0].reshape(grp * TQ, HEAD)
        m = None
        acc = None
        for c0 in range(start, stop, kc):
            s = lax.dot_general(q, k_ref[0, 0, c0:c0 + kc, :], (((1,), (1,)), ((), ())), preferred_element_type=F32)
            m_c = jnp.max(s, axis=-1, keepdims=True)
            m_new = m_c if m is None else jnp.maximum(m, m_c)
            pv = jnp.dot(jnp.exp2(s - m_new).astype(MXU_DT), v_ref[0, 0, c0:c0 + kc, :], preferred_element_type=F32)
            acc = pv if m is None else acc * jnp.exp2(m - m_new) + pv
            m = m_new
        o = acc[:, 0:HEAD] / acc[:, HEAD:HEAD + 1]
        for g in range(grp):
            o_ref[:, g * HEAD:(g + 1) * HEAD] = o[g * TQ:(g + 1) * TQ, :].astype(o_ref.dtype)

    qi = pl.program_id(2)
    ltot = k_ref.shape[2]

    @pl.when(qi < n_lat_qblocks)
    def _():
        attend(0, ltot, key_chunk)

    @pl.when(qi >= n_lat_qblocks)
    def _():
        attend(lat_len, ltot, ltot - lat_len)


def _attention(qh, kh, vh, *, lat_len, with_ctx, n_rows):
    batch, _, ltot, _ = qh.shape
    grp = AQ_HEADS // AKV_HEADS
    n_lat_q = lat_len // TQ
    n_q = ltot // TQ if with_ctx else n_lat_q
    n_ctx_q = (ltot - lat_len) // TQ

    def orow(b, qi):
        return jnp.where(qi < n_lat_q, b * n_lat_q + qi, batch * n_lat_q + b * n_ctx_q + (qi - n_lat_q))

    return pl.pallas_call(
        functools.partial(_attn_kernel, n_lat_qblocks=n_lat_q, lat_len=lat_len,
                          key_chunk=_key_chunk(ltot, ATTN_KEY_CHUNK_CAP)),
        grid=(batch, AKV_HEADS, n_q),
        in_specs=[pl.BlockSpec((1, grp, TQ, HEAD), lambda b, g, qi: (b, g, qi, 0)),
                  pl.BlockSpec((1, 1, ltot, HEAD), lambda b, g, qi: (b, g, 0, 0)),
                  pl.BlockSpec((1, 1, ltot, 2 * HEAD), lambda b, g, qi: (b, g, 0, 0))],
        out_specs=pl.BlockSpec((TQ, grp * HEAD), lambda b, g, qi: (orow(b, qi), g)),
        out_shape=jax.ShapeDtypeStruct((n_rows, AQ_W), MXU_DT),
        compiler_params=_cp(("parallel", "parallel", "arbitrary")),
        name="attention",
    )(qh, kh, vh)


def _plain_matmul_kernel(x_ref, w_ref, o_ref):
    o_ref[...] = _dot(x_ref[...], w_ref[...]).astype(o_ref.dtype)


def _fnet_channel_dft(p, cs, *, row_off, n_rows):
    return pl.pallas_call(
        _plain_matmul_kernel,
        grid=(n_rows // TM,),
        in_specs=[pl.BlockSpec((TM, FN_W), lambda i: (row_off // TM + i, C_FN // FN_W)),
                  pl.BlockSpec((FN_W, 2 * FN_W), lambda i: (0, 0))],
        out_specs=pl.BlockSpec((TM, 2 * FN_W), lambda i: (i, 0)),
        out_shape=jax.ShapeDtypeStruct((n_rows, 2 * FN_W), MXU_DT),
        compiler_params=_cp(("parallel",)),
        name="fnet_channel_dft",
    )(p, cs)


def _fnet_pos_kernel(*refs, batch):
    cos_ref, sin_ref = refs[0], refs[1]
    y_refs = refs[2:2 + batch]
    w_ref = refs[2 + batch]
    o_ref = refs[3 + batch]
    acc_ref = refs[4 + batch]
    kk = pl.program_id(1)

    @pl.when(kk == 0)
    def _():
        acc_ref[...] = jnp.zeros_like(acc_ref)

    c = cos_ref[...]
    s = sin_ref[...]
    for b in range(batch):
        y = y_refs[b][...]
        acc_ref[b] += (jnp.dot(c, y[:, 0:FN_W], preferred_element_type=F32)
                       - jnp.dot(s, y[:, FN_W:2 * FN_W], preferred_element_type=F32))

    @pl.when(kk == pl.num_programs(1) - 1)
    def _():
        for b in range(batch):
            o_ref[b] = _dot(acc_ref[b], w_ref[...]).astype(o_ref.dtype)


def _fnet_pos_dft(y, cos_t, sin_t, fnet_w, layer, *, batch, seq_len, row_off):
    t = min(1024, seq_len)
    nt = seq_len // t
    y_specs = [pl.BlockSpec((t, 2 * FN_W), functools.partial(lambda i, k, b: (row_off // t + b * nt + k, 0), b=b))
               for b in range(batch)]
    return pl.pallas_call(
        functools.partial(_fnet_pos_kernel, batch=batch),
        grid=(nt, nt),
        in_specs=[pl.BlockSpec((t, t), lambda i, k: (i, k)),
                  pl.BlockSpec((t, t), lambda i, k: (i, k))] + y_specs +
                 [pl.BlockSpec((None, FN_W, FN_W), lambda i, k: (layer, 0, 0))],
        out_specs=pl.BlockSpec((batch, t, FN_W), lambda i, k: (0, i, 0)),
        out_shape=jax.ShapeDtypeStruct((batch, seq_len, FN_W), MXU_DT),
        scratch_shapes=[pltpu.VMEM((batch, t, FN_W), F32)],
        compiler_params=_cp(("parallel", "arbitrary")),
        name="fnet_pos_dft",
    )(cos_t, sin_t, *([y] * batch), fnet_w)


def _dft_tables(n):
    idx = jnp.arange(n, dtype=jnp.int32)
    ang = ((idx[:, None] * idx[None, :]) % n).astype(F32) * (2.0 * math.pi / n)
    scale = n ** -0.5
    return jnp.cos(ang) * scale, jnp.sin(ang) * scale


def _fft_stage1_kernel(y_ref, c_ref, s_ref, twc_ref, tws_ref, o_ref, *, j1_per_step):
    lane = lax.broadcasted_iota(jnp.int32, (HEAD, HEAD), 1)
    c = c_ref[...]
    s = s_ref[...]
    for t in range(j1_per_step):
        j1 = pl.program_id(1) * j1_per_step + t
        yc = y_ref[:, t * 2 * FN_W:t * 2 * FN_W + FN_W]
        ys = y_ref[:, t * 2 * FN_W + FN_W:(t + 1) * 2 * FN_W]
        ar = jnp.dot(c, yc, preferred_element_type=F32) - jnp.dot(s, ys, preferred_element_type=F32)
        ai = -(jnp.dot(s, yc, preferred_element_type=F32) + jnp.dot(c, ys, preferred_element_type=F32))
        tc = _col(twc_ref[...], lane, j1)
        ts = _col(tws_ref[...], lane, j1)
        o_ref[t, :, 0:FN_W] = (ar * tc + ai * ts).astype(o_ref.dtype)
        o_ref[t, :, FN_W:2 * FN_W] = (ai * tc - ar * ts).astype(o_ref.dtype)


def _fft_stage2_kernel(b_ref, c_ref, s_ref, w_ref, o_ref, *, k2_per_step):
    c = c_ref[...]
    s = s_ref[...]
    for t in range(k2_per_step):
        br = b_ref[:, t * 2 * FN_W:t * 2 * FN_W + FN_W]
        bi = b_ref[:, t * 2 * FN_W + FN_W:(t + 1) * 2 * FN_W]
        xr = jnp.dot(c, br, preferred_element_type=F32) + jnp.dot(s, bi, preferred_element_type=F32)
        o_ref[:, t * FN_W:(t + 1) * FN_W] = _dot(xr, w_ref[...]).astype(o_ref.dtype)


def _fnet_pos_fft(y, tabs, fnet_w, layer, *, batch, seq_len):
    c2, s2, twc, tws, c1, s1 = tabs
    l1 = seq_len // HEAD
    per = min(8, l1)
    yv = y.reshape(y.shape[0] // l1, l1 * 2 * FN_W)
    b_arr = pl.pallas_call(
        functools.partial(_fft_stage1_kernel, j1_per_step=per),
        grid=(batch, l1 // per),
        in_specs=[pl.BlockSpec((HEAD, per * 2 * FN_W), lambda b, j: (b, j)),
                  pl.BlockSpec((HEAD, HEAD), lambda b, j: (0, 0)),
                  pl.BlockSpec((HEAD, HEAD), lambda b, j: (0, 0)),
                  pl.BlockSpec((HEAD, HEAD), lambda b, j: (0, 0)),
                  pl.BlockSpec((HEAD, HEAD), lambda b, j: (0, 0))],
        out_specs=pl.BlockSpec((per, HEAD, 2 * FN_W), lambda b, j: (b * (l1 // per) + j, 0, 0)),
        out_shape=jax.ShapeDtypeStruct((batch * l1, HEAD, 2 * FN_W), MXU_DT),
        compiler_params=_cp(("parallel", "parallel")),
        name="fnet_fft_stage1",
    )(yv, c2, s2, twc, tws)
    bv = b_arr.reshape(batch, l1, HEAD * 2 * FN_W)
    out = pl.pallas_call(
        functools.partial(_fft_stage2_kernel, k2_per_step=8),
        grid=(batch, HEAD // 8),
        in_specs=[pl.BlockSpec((None, l1, 8 * 2 * FN_W), lambda b, k: (b, 0, k)),
                  pl.BlockSpec((l1, l1), lambda b, k: (0, 0)),
                  pl.BlockSpec((l1, l1), lambda b, k: (0, 0)),
                  pl.BlockSpec((None, FN_W, FN_W), lambda b, k: (layer, 0, 0))],
        out_specs=pl.BlockSpec((None, l1, 8 * FN_W), lambda b, k: (b, 0, k)),
        out_shape=jax.ShapeDtypeStruct((batch, l1, HEAD * FN_W), MXU_DT),
        compiler_params=_cp(("parallel", "parallel")),
        name="fnet_fft_stage2",
    )(bv, c1, s1, fnet_w)
    return out.reshape(batch * seq_len, FN_W)


def _fft_tables(seq_len):
    l1 = seq_len // HEAD
    c2, s2 = _dft_tables(HEAD)
    c1, s1 = _dft_tables(l1)
    k2 = jnp.arange(HEAD, dtype=jnp.int32)[:, None]
    j1 = jnp.arange(HEAD, dtype=jnp.int32)[None, :]
    ang = ((k2 * j1) % seq_len).astype(F32) * (2.0 * math.pi / seq_len)
    return (c2.astype(MXU_DT), s2.astype(MXU_DT), jnp.cos(ang), jnp.sin(ang), c1.astype(MXU_DT), s1.astype(MXU_DT))


def _out_proj_kernel(x_ref, dn_ref, at_ref, fn_ref, w_ref, mod_ref, g_ref, b_ref, o_ref, *, alpha):
    acc = jnp.dot(dn_ref[...], w_ref[0:DN_W, :], preferred_element_type=F32)
    acc += jnp.dot(at_ref[...], w_ref[DN_W:DN_W + AQ_W, :], preferred_element_type=F32)
    acc += _dot(fn_ref[...], w_ref[DN_W + AQ_W:, :])
    y = alpha * x_ref[...] + mod_ref[2:3, :] * acc
    o_ref[...] = _ln(y) * g_ref[0:1, :] + b_ref[0:1, :]


def _out_proj(x, a_dn, a_at, a_fn, w_out, mod, g, b, layer, *, seg_of_tile, n_rows, alpha):
    d = x.shape[1]
    tm = RB
    seg = lambda i: seg_of_tile(i // (TM // tm))
    return pl.pallas_call(
        functools.partial(_out_proj_kernel, alpha=alpha),
        grid=(n_rows // tm,),
        in_specs=[pl.BlockSpec((tm, d), lambda i: (i, 0)),
                  pl.BlockSpec((tm, DN_W), lambda i: (i, 0)),
                  pl.BlockSpec((tm, AQ_W), lambda i: (i, 0)),
                  pl.BlockSpec((tm, FN_W), lambda i: (i, 0)),
                  pl.BlockSpec((None, DN_W + AQ_W + FN_W, d), lambda i: (layer, 0, 0)),
                  pl.BlockSpec((None, 6, d), lambda i: (seg(i), 0, 0)),
                  pl.BlockSpec((None, 1, d), lambda i: (layer, 0, 0)),
                  pl.BlockSpec((None, 1, d), lambda i: (layer, 0, 0))],
        out_specs=pl.BlockSpec((tm, d), lambda i: (i, 0)),
        out_shape=jax.ShapeDtypeStruct((n_rows, d), F32),
        compiler_params=_cp(("parallel",)),
        name="out_proj",
    )(x, a_dn, a_at, a_fn, w_out, mod, g, b)


def _ffn_kernel(x_ref, mod_ref, wg_ref, wu_ref, wd_ref, g_ref, b_ref, o_ref, h_ref, acc_ref, *, alpha):
    f = pl.program_id(1)

    @pl.when(f == 0)
    def _():
        h = _ln(x_ref[...]) * (1.0 + mod_ref[4:5, :]) + mod_ref[3:4, :]
        h_ref[...] = h.astype(MXU_DT)
        acc_ref[...] = jnp.zeros_like(acc_ref)

    h = h_ref[...]
    gate = jnp.dot(h, wg_ref[...], preferred_element_type=F32)
    up = jnp.dot(h, wu_ref[...], preferred_element_type=F32)
    acc_ref[...] += jnp.dot((_silu(gate) * up).astype(MXU_DT), wd_ref[...], preferred_element_type=F32)

    @pl.when(f == pl.num_programs(1) - 1)
    def _():
        y = alpha * x_ref[...] + mod_ref[5:6, :] * acc_ref[...]
        o_ref[...] = _ln(y) * g_ref[0:1, :] + b_ref[0:1, :]


def _ffn(x, mod, wg, wu, wd, g, b, layer, li, *, seg_of_tile, n_rows, alpha, tf):
    d = x.shape[1]
    ff = wg.shape[2]
    return pl.pallas_call(
        functools.partial(_ffn_kernel, alpha=alpha),
        grid=(n_rows // TM, ff // tf),
        in_specs=[pl.BlockSpec((TM, d), lambda i, f: (i, 0)),
                  pl.BlockSpec((None, 6, d), lambda i, f: (seg_of_tile(i), 0, 0)),
                  pl.BlockSpec((None, d, tf), lambda i, f: (li, 0, f)),
                  pl.BlockSpec((None, d, tf), lambda i, f: (li, 0, f)),
                  pl.BlockSpec((None, tf, d), lambda i, f: (li, f, 0)),
                  pl.BlockSpec((None, 1, d), lambda i, f: (layer, 0, 0)),
                  pl.BlockSpec((None, 1, d), lambda i, f: (layer, 0, 0))],
        out_specs=pl.BlockSpec((TM, d), lambda i, f: (i, 0)),
        out_shape=jax.ShapeDtypeStruct((n_rows, d), F32),
        scratch_shapes=[pltpu.VMEM((TM, d), MXU_DT), pltpu.VMEM((TM, d), F32)],
        compiler_params=_cp(("parallel", "arbitrary")),
        name="ffn",
    )(x, mod, wg, wu, wd, g, b)


def _router_kernel(x_ref, mod_ref, rw_ref, h_ref, idx_ref, gate_ref):
    h = _ln(x_ref[...]) * (1.0 + mod_ref[4:5, :]) + mod_ref[3:4, :]
    h_ref[...] = h.astype(h_ref.dtype)
    logits = jnp.dot(h, rw_ref[...], precision=HI, preferred_element_type=F32)
    lane = lax.broadcasted_iota(jnp.int32, logits.shape, 1)
    neg = jnp.float32(-jnp.inf)
    l1 = jnp.where(lane < N_EXPERTS, logits, neg)
    m1 = jnp.max(l1, axis=-1, keepdims=True)
    i1 = jnp.min(jnp.where(l1 == m1, lane, HEAD), axis=-1, keepdims=True)
    l2 = jnp.where(lane == i1, neg, l1)
    m2 = jnp.max(l2, axis=-1, keepdims=True)
    i2 = jnp.min(jnp.where(l2 == m2, lane, HEAD), axis=-1, keepdims=True)
    e = jnp.exp(m2 - m1)
    g1 = 1.0 / (1.0 + e)
    g2 = e / (1.0 + e)
    idx_ref[...] = jnp.where(lane == 0, i1, jnp.where(lane == 1, i2, 0))
    gate_ref[...] = jnp.where(lane == 0, g1, jnp.where(lane == 1, g2, 0.0))


def _router(x, mod, rw, *, seg_of_tile, n_rows):
    d = x.shape[1]
    return pl.pallas_call(
        _router_kernel,
        grid=(n_rows // TM,),
        in_specs=[pl.BlockSpec((TM, d), lambda i: (i, 0)),
                  pl.BlockSpec((None, 6, d), lambda i: (seg_of_tile(i), 0, 0)),
                  pl.BlockSpec((d, HEAD), lambda i: (0, 0))],
        out_specs=[pl.BlockSpec((TM, d), lambda i: (i, 0)),
                   pl.BlockSpec((TM, HEAD), lambda i: (i, 0)),
                   pl.BlockSpec((TM, HEAD), lambda i: (i, 0))],
        out_shape=[jax.ShapeDtypeStruct((n_rows, d), MXU_DT),
                   jax.ShapeDtypeStruct((n_rows, HEAD), jnp.int32),
                   jax.ShapeDtypeStruct((n_rows, HEAD), F32)],
        compiler_params=_cp(("parallel",)),
        name="router",
    )(x, mod, rw)


def _moe_kernel(te_ref, nv_ref, h_ref, wg_ref, wu_ref, wd_ref, o_ref, acc_ref):
    t = pl.program_id(0)
    f = pl.program_id(1)
    last = pl.num_programs(1) - 1
    valid = t < nv_ref[0]

    @pl.when(jnp.logical_and(valid, f == 0))
    def _():
        acc_ref[...] = jnp.zeros_like(acc_ref)

    @pl.when(valid)
    def _():
        h = h_ref[...]
        gate = _dot(h, wg_ref[...])
        up = _dot(h, wu_ref[...])
        acc_ref[...] += _dot(_silu(gate) * up, wd_ref[...])

    @pl.when(jnp.logical_and(valid, f == last))
    def _():
        o_ref[...] = acc_ref[...].astype(o_ref.dtype)

    @pl.when(jnp.logical_and(jnp.logical_not(valid), f == last))
    def _():
        o_ref[...] = jnp.zeros_like(o_ref)


def _moe_experts(tile_expert, n_valid, h_sorted, wg, wu, wd, mi, *, tf):
    mp, d = h_sorted.shape
    ff = wg.shape[3]
    nf = ff // tf

    def fidx(t, f, nv):
        return jnp.where(t < nv[0], f, nf - 1)

    grid_spec = pltpu.PrefetchScalarGridSpec(
        num_scalar_prefetch=2,
        grid=(mp // TM_MOE, nf),
        in_specs=[pl.BlockSpec((TM_MOE, d), lambda t, f, te, nv: (t, 0)),
                  pl.BlockSpec((None, None, d, tf), lambda t, f, te, nv: (mi, te[t], 0, fidx(t, f, nv))),
                  pl.BlockSpec((None, None, d, tf), lambda t, f, te, nv: (mi, te[t], 0, fidx(t, f, nv))),
                  pl.BlockSpec((None, None, tf, d), lambda t, f, te, nv: (mi, te[t], fidx(t, f, nv), 0))],
        out_specs=pl.BlockSpec((TM_MOE, d), lambda t, f, te, nv: (t, 0)),
        scratch_shapes=[pltpu.VMEM((TM_MOE, d), F32)],
    )
    return pl.pallas_call(
        _moe_kernel,
        grid_spec=grid_spec,
        out_shape=jax.ShapeDtypeStruct((mp, d), MXU_DT),
        compiler_params=_cp(("arbitrary", "arbitrary")),
        name="moe_experts",
    )(tile_expert, n_valid, h_sorted, wg, wu, wd)


def _moe_combine_kernel(x_ref, y1_ref, y2_ref, gate_ref, mod_ref, g_ref, b_ref, o_ref, *, alpha):
    gates = gate_ref[...]
    y = gates[:, 0:1] * y1_ref[...].astype(F32) + gates[:, 1:2] * y2_ref[...].astype(F32)
    z = alpha * x_ref[...] + mod_ref[5:6, :] * y
    o_ref[...] = _ln(z) * g_ref[0:1, :] + b_ref[0:1, :]


def _moe_combine(x, y12, gates, mod, g, b, layer, *, seg_of_tile, n_rows, alpha):
    d = x.shape[1]
    tm = RB
    nt = n_rows // tm
    seg = lambda i: seg_of_tile(i // (TM // tm))
    return pl.pallas_call(
        functools.partial(_moe_combine_kernel, alpha=alpha),
        grid=(nt,),
        in_specs=[pl.BlockSpec((tm, d), lambda i: (i, 0)),
                  pl.BlockSpec((tm, d), lambda i: (i, 0)),
                  pl.BlockSpec((tm, d), lambda i: (nt + i, 0)),
                  pl.BlockSpec((tm, HEAD), lambda i: (i, 0)),
                  pl.BlockSpec((None, 6, d), lambda i: (seg(i), 0, 0)),
                  pl.BlockSpec((None, 1, d), lambda i: (layer, 0, 0)),
                  pl.BlockSpec((None, 1, d), lambda i: (layer, 0, 0))],
        out_specs=pl.BlockSpec((tm, d), lambda i: (i, 0)),
        out_shape=jax.ShapeDtypeStruct((n_rows, d), F32),
        compiler_params=_cp(("parallel",)),
        name="moe_combine",
    )(x, y12, y12, gates, mod, g, b)


def _moe_layer(x, mod, rw, wg, wu, wd, g, b, layer, mi, *, seg_of_tile, n_rows, alpha, tf):
    h, idx, gates = _router(x, mod, rw, seg_of_tile=seg_of_tile, n_rows=n_rows)
    na = 2 * n_rows
    mp = na + N_EXPERTS * TM_MOE
    e_flat = idx[:, 0:2].reshape(na)
    onehot = (e_flat[:, None] == jnp.arange(N_EXPERTS, dtype=jnp.int32)[None, :]).astype(jnp.int32)
    csum = jnp.cumsum(onehot, axis=0)
    rank = jnp.sum((csum - onehot) * onehot, axis=1)
    counts = csum[-1]
    padded = ((counts + TM_MOE - 1) // TM_MOE) * TM_MOE
    ends = jnp.cumsum(padded)
    dest = (ends - padded)[e_flat] + rank
    src_token = jnp.zeros((mp,), jnp.int32).at[dest].set(jnp.arange(na, dtype=jnp.int32) // 2)
    tile_start = jnp.arange(mp // TM_MOE, dtype=jnp.int32) * TM_MOE
    tile_expert = jnp.sum((ends[None, :] <= tile_start[:, None]).astype(jnp.int32), axis=1)
    tile_expert = jnp.minimum(tile_expert, N_EXPERTS - 1)
    n_valid = (ends[-1] // TM_MOE).astype(jnp.int32).reshape(1)
    h_sorted = jnp.take(h, src_token, axis=0, mode="clip")
    y_sorted = _moe_experts(tile_expert, n_valid, h_sorted, wg, wu, wd, mi, tf=tf)
    y12 = jnp.take(y_sorted, dest.reshape(n_rows, 2).T.reshape(na), axis=0, mode="clip")
    return _moe_combine(x, y12, gates, mod, g, b, layer, seg_of_tile=seg_of_tile, n_rows=n_rows, alpha=alpha)


def _rope_tables(seq_len):
    axis_dim = HEAD // 2
    inv = ROPE_THETA ** (-jnp.arange(0, axis_dim, 2, dtype=F32) / axis_dim)
    t = jnp.arange(seq_len, dtype=jnp.int32)
    row = (t // GRID_W).astype(F32)[:, None] * inv
    col = (t % GRID_W).astype(F32)[:, None] * inv
    cos = jnp.concatenate([jnp.cos(row), jnp.cos(row), jnp.cos(col), jnp.cos(col)], axis=-1)
    sin = jnp.concatenate([-jnp.sin(row), jnp.sin(row), -jnp.sin(col), jnp.sin(col)], axis=-1)
    return cos, sin


def _pad_rows8(v):
    return jnp.zeros((8, v.shape[-1]), F32).at[0].set(v.astype(F32))


def kernel(x, c, ctx, c_ctx, w_mod, b_mod, w_in, dn_conv, dn_a_log, dn_dt_bias, dn_norm, attn_q_norm, attn_k_norm,
           fnet_w, w_out, ln1_g, ln1_b, ln2_g, ln2_b, ffn_w_gate, ffn_w_up, ffn_w_down, router, moe_w_gate,
           moe_w_up, moe_w_down):
    batch, seq, d = x.shape
    ctx_len = ctx.shape[1]
    depth = w_mod.shape[0]
    ff = ffn_w_gate.shape[2]
    assert ctx_len == RB and seq % TM == 0 and (batch * ctx_len) % TM == 0 and TM % RB == 0
    alpha = (2 * depth) ** 0.25
    n_lat = batch * seq
    n_all = n_lat + batch * ctx_len
    tiles_per_seq = seq // TM
    blocks_per_seq = seq // RB
    seg_of_tile = lambda i: jnp.minimum(i // tiles_per_seq, batch)
    tf = 512 if ff % 512 == 0 else ff

    w_in_big = jnp.concatenate([w_in[:, :, :4 * DN_W], w_in[:, :, 4 * DN_W + 4 * DN_HEADS:]], axis=-1).astype(MXU_DT)
    w_in_small = jnp.pad(w_in[:, :, 4 * DN_W:4 * DN_W + 4 * DN_HEADS],
                         ((0, 0), (0, 0), (0, HEAD - 4 * DN_HEADS))).astype(MXU_DT)
    conv_w = jnp.pad(dn_conv, ((0, 0), (0, 8 - CONV_K), (0, 0)))
    w_out_c = w_out.astype(MXU_DT)
    fnet_w_c = fnet_w.astype(MXU_DT)
    ffn_wg, ffn_wu, ffn_wd = ffn_w_gate.astype(MXU_DT), ffn_w_up.astype(MXU_DT), ffn_w_down.astype(MXU_DT)
    router_p = jnp.pad(router, ((0, 0), (0, 0), (0, HEAD - N_EXPERTS)))
    b_mod3 = b_mod[:, None, :]
    ln1_g3, ln1_b3, ln2_g3, ln2_b3 = ln1_g[:, None, :], ln1_b[:, None, :], ln2_g[:, None, :], ln2_b[:, None, :]

    rope_cos, rope_sin = _rope_tables(seq)
    lat_tabs = _fft_tables(seq)
    ctx_cos, ctx_sin = (t.astype(MXU_DT) for t in _dft_tables(ctx_len))
    ch_cos, ch_sin = _dft_tables(HEAD)
    eye_g = jnp.eye(FN_GROUPS, dtype=F32)
    ch_cs = jnp.concatenate([jnp.kron(eye_g, ch_cos), jnp.kron(eye_g, ch_sin)], axis=-1).astype(MXU_DT)

    c8 = jnp.zeros((8, d), F32).at[0:batch].set(c).at[batch].set(c_ctx)
    xs = jnp.concatenate([x.reshape(n_lat, d), ctx.reshape(batch * ctx_len, d)], axis=0)

    for layer in range(depth):
        need_ctx = layer < depth - 1
        rows_out = n_all if need_ctx else n_lat
        mod = _modulation(c8, w_mod, b_mod3, layer).reshape(8, 6, d)

        p = _ln_mod_matmul(xs, mod, w_in_big, layer, seg_of_tile=seg_of_tile, shift_row=0, tn=P_COLS // 4)
        ps = _ln_mod_matmul(xs, mod, w_in_small, layer, seg_of_tile=seg_of_tile, shift_row=0, tn=HEAD)

        qkv = _dn_conv(p, conv_w, layer, blocks_per_seq=blocks_per_seq, n_lat_blocks=batch * blocks_per_seq)
        par = (jnp.zeros((8, HEAD), F32)
               .at[0, 2 * DN_HEADS:4 * DN_HEADS].set(dn_dt_bias[layer].reshape(-1))
               .at[1, 2 * DN_HEADS:4 * DN_HEADS].set(dn_a_log[layer].reshape(-1)))
        prep = _dn_prep(qkv, ps, par, batch=batch, blocks_per_seq=blocks_per_seq)
        o_f, o_b = _dn_scan(prep)
        a_dn = _dn_out(o_f, o_b, p, _pad_rows8(dn_norm[layer]), batch=batch, blocks_per_seq=blocks_per_seq,
                       nblk=rows_out // RB)

        qh, kh, vh = _attn_prep(p, rope_cos, rope_sin, _pad_rows8(attn_q_norm[layer]), _pad_rows8(attn_k_norm[layer]),
                                batch=batch, blocks_per_seq=blocks_per_seq)
        a_at = _attention(qh, kh, vh, lat_len=seq, with_ctx=need_ctx, n_rows=rows_out)

        y_lat = _fnet_channel_dft(p, ch_cs, row_off=0, n_rows=n_lat)
        a_fn = _fnet_pos_fft(y_lat, lat_tabs, fnet_w_c, layer, batch=batch, seq_len=seq)
        if need_ctx:
            y_ctx = _fnet_channel_dft(p, ch_cs, row_off=n_lat, n_rows=batch * ctx_len)
            a_fn_c = _fnet_pos_dft(y_ctx, ctx_cos, ctx_sin, fnet_w_c, layer, batch=batch, seq_len=ctx_len, row_off=0)
            a_fn = jnp.concatenate([a_fn, a_fn_c.reshape(batch * ctx_len, FN_W)], axis=0)

        xs = _out_proj(xs, a_dn, a_at, a_fn, w_out_c, mod, ln1_g3, ln1_b3, layer, seg_of_tile=seg_of_tile,
                       n_rows=rows_out, alpha=alpha)

        li = layer // 2
        if layer % 2 == 0:
            xs = _ffn(xs, mod, ffn_wg, ffn_wu, ffn_wd, ln2_g3, ln2_b3, layer, li, seg_of_tile=seg_of_tile,
                      n_rows=rows_out, alpha=alpha, tf=tf)
        else:
            xs = _moe_layer(xs, mod, router_p[li], moe_w_gate, moe_w_up, moe_w_down, ln2_g3, ln2_b3, layer, li,
                            seg_of_tile=seg_of_tile, n_rows=rows_out, alpha=alpha,
                            tf=TF_MOE if ff % TF_MOE == 0 else ff)
    return xs[:n_lat].reshape(batch, seq, d)
```

```python
import functools
import math

import jax
import jax.numpy as jnp
from jax import lax
from jax.experimental import pallas as pl
from jax.experimental.pallas import tpu as pltpu

F32 = jnp.float32
MXU_DT = jnp.bfloat16
HI = lax.Precision.HIGHEST

EPS = 1e-6
HEAD = 128
DN_HEADS = 8
AQ_HEADS = 4
AKV_HEADS = 2
FN_GROUPS = 4
CHUNK = 64
CONV_K = 5
GRID_W = 64
ROPE_THETA = 10000.0
N_EXPERTS = 8
DN_W = DN_HEADS * HEAD
AQ_W = AQ_HEADS * HEAD
AKV_W = AKV_HEADS * HEAD
FN_W = FN_GROUPS * HEAD
C_DNQ, C_DNK, C_DNV = 0, DN_W, 2 * DN_W
C_GATE = 3 * DN_W
C_AQ = 4 * DN_W
C_AK = C_AQ + AQ_W
C_AV = C_AK + AKV_W
C_FN = C_AV + AKV_W
P_COLS = C_FN + FN_W

TM = 512
TM_MOE = 768
TF_MOE = 256
RB = 256
DN_HPS = 4
TQ = 256
ATTN_KEY_CHUNK_CAP = 1536
V7X_VMEM_LIMIT = 56 * 1024 * 1024


def _cp(sem, vmem=V7X_VMEM_LIMIT):
    return pltpu.CompilerParams(dimension_semantics=sem, vmem_limit_bytes=vmem)


def _silu(x):
    return x * jax.nn.sigmoid(x)


def _ln(x):
    mu = jnp.mean(x, axis=-1, keepdims=True)
    xc = x - mu
    var = jnp.mean(xc * xc, axis=-1, keepdims=True)
    return xc * lax.rsqrt(var + EPS)


def _dot(a, b):
    return jnp.dot(a.astype(MXU_DT), b.astype(MXU_DT), preferred_element_type=F32)


def _dot_nt(a, b):
    return lax.dot_general(a.astype(MXU_DT), b.astype(MXU_DT), (((1,), (1,)), ((), ())),
                           preferred_element_type=F32)


def _mod_kernel(c_ref, w_ref, b_ref, o_ref):
    o_ref[...] = _dot(_silu(c_ref[...]), w_ref[...]) + b_ref[...]


def _modulation(c8, w_mod, b_mod, layer):
    d = c8.shape[1]
    n = w_mod.shape[2]
    tn = d // 2
    return pl.pallas_call(
        _mod_kernel,
        grid=(n // tn,),
        in_specs=[pl.BlockSpec((8, d), lambda j: (0, 0)),
                  pl.BlockSpec((None, d, tn), lambda j: (layer, 0, j)),
                  pl.BlockSpec((None, 1, tn), lambda j: (layer, 0, j))],
        out_specs=pl.BlockSpec((8, tn), lambda j: (0, j)),
        out_shape=jax.ShapeDtypeStruct((8, n), F32),
        compiler_params=_cp(("parallel",)),
        name="modulation",
    )(c8, w_mod, b_mod)


def _ln_mod_matmul_kernel(x_ref, mod_ref, w_ref, o_ref, h_ref, *, shift_row):
    @pl.when(pl.program_id(1) == 0)
    def _():
        h = _ln(x_ref[...]) * (1.0 + mod_ref[shift_row + 1:shift_row + 2, :]) + mod_ref[shift_row:shift_row + 1, :]
        h_ref[...] = h.astype(MXU_DT)

    o_ref[...] = jnp.dot(h_ref[...], w_ref[...], preferred_element_type=F32)


def _ln_mod_matmul(x, mod, w, layer, *, seg_of_tile, shift_row, tn):
    r, d = x.shape
    n = w.shape[2]
    return pl.pallas_call(
        functools.partial(_ln_mod_matmul_kernel, shift_row=shift_row),
        grid=(r // TM, n // tn),
        in_specs=[pl.BlockSpec((TM, d), lambda i, j: (i, 0)),
                  pl.BlockSpec((None, 6, d), lambda i, j: (seg_of_tile(i), 0, 0)),
                  pl.BlockSpec((None, d, tn), lambda i, j: (layer, 0, j))],
        out_specs=pl.BlockSpec((TM, tn), lambda i, j: (i, j)),
        out_shape=jax.ShapeDtypeStruct((r, n), F32),
        scratch_shapes=[pltpu.VMEM((TM, d), MXU_DT)],
        compiler_params=_cp(("parallel", "arbitrary")),
        name="in_proj",
    )(x, mod, w)


def _short_conv_silu(prev_ref, x_ref, next_ref, w_ref, buf_ref, prev_ok, next_ok):
    buf_ref[0:8, :] = jnp.where(prev_ok, prev_ref[...], 0.0)
    buf_ref[8:8 + RB, :] = x_ref[...]
    buf_ref[8 + RB:16 + RB, :] = jnp.where(next_ok, next_ref[...], 0.0)
    pad = CONV_K // 2
    y = buf_ref[8 - pad:8 - pad + RB, :] * w_ref[0:1, :]
    for tap in range(1, CONV_K):
        y = y + buf_ref[8 - pad + tap:8 - pad + tap + RB, :] * w_ref[tap:tap + 1, :]
    return _silu(y)


def _l2norm(y):
    return y * lax.rsqrt(jnp.sum(y * y, axis=-1, keepdims=True) + EPS)


def _dn_scan_operands():
    return (((2 * CHUNK, HEAD), MXU_DT), ((CHUNK, HEAD), F32), ((HEAD, CHUNK), MXU_DT),
            ((CHUNK, CHUNK), MXU_DT), ((8, HEAD), F32))

def _col(x, lane, idx):
    return jnp.sum(jnp.where(lane == idx, x, 0.0), axis=-1, keepdims=True)


def _dn_prep_kernel(qp_ref, qx_ref, qn_ref, qw_ref, kp_ref, kx_ref, kn_ref, kw_ref, vp_ref, vx_ref, vn_ref, vw_ref,
                    ps_ref, par_ref, wq_f, u_f, kg_f, qk_f, dl_f, wq_b, u_b, kg_b, qk_b, dl_b, buf_ref,
                    *, blocks_per_seq, n_lat_blocks):
    i = pl.program_id(0)
    li = i % blocks_per_seq
    is_lat = i < n_lat_blocks
    prev_ok = jnp.logical_and(is_lat, li > 0)
    next_ok = jnp.logical_and(is_lat, li < blocks_per_seq - 1)
    q_all = _short_conv_silu(qp_ref, qx_ref, qn_ref, qw_ref, buf_ref.at[0], prev_ok, next_ok)
    k_all = _short_conv_silu(kp_ref, kx_ref, kn_ref, kw_ref, buf_ref.at[1], prev_ok, next_ok)
    v_all = _short_conv_silu(vp_ref, vx_ref, vn_ref, vw_ref, buf_ref.at[2], prev_ok, next_ok)
    h0 = pl.program_id(1) * DN_HPS
    nchunk = RB // CHUNK
    ps = ps_ref[...]
    lane = lax.broadcasted_iota(jnp.int32, (RB, HEAD), 1)
    rowc = lax.broadcasted_iota(jnp.int32, (RB, HEAD), 0) % CHUNK
    beta_all = jax.nn.sigmoid(ps)
    z = ps + par_ref[0:1, :]
    dt = jnp.maximum(z, 0.0) + jnp.log(1.0 + jnp.exp(-jnp.abs(z)))
    g_all = -jnp.exp(par_ref[1:2, :]) * dt
    gf = g_all
    gb = g_all
    s = 1
    while s < CHUNK:
        gf = gf + jnp.where(rowc >= s, pltpu.roll(gf, s, 0), 0.0)
        gb = gb + jnp.where(rowc < CHUNK - s, pltpu.roll(gb, RB - s, 0), 0.0)
        s *= 2
    tot_all = gf + gb - g_all

    r2 = lax.broadcasted_iota(jnp.int32, (RB, RB), 0)
    c2 = lax.broadcasted_iota(jnp.int32, (RB, RB), 1)
    eye = (r2 == c2).astype(F32)
    same = (r2 // CHUNK) == (c2 // CHUNK)
    incl = (jnp.logical_and(same, r2 >= c2), jnp.logical_and(same, r2 <= c2))
    level_masks = []
    bs = 2
    while bs < CHUNK:
        level_masks.append(jnp.logical_and((r2 // (2 * bs)) == (c2 // (2 * bs)), (r2 // bs) != (c2 // bs)))
        bs *= 2

    probs = [(hh, d) for hh in range(DN_HPS) for d in range(2)]
    qs = [_l2norm(q_all[:, hh * HEAD:(hh + 1) * HEAD]) * (HEAD ** -0.5) for hh in range(DN_HPS)]
    ks = [_l2norm(k_all[:, hh * HEAD:(hh + 1) * HEAD]) for hh in range(DN_HPS)]
    vs = [v_all[:, hh * HEAD:(hh + 1) * HEAD] for hh in range(DN_HPS)]
    qkk = [_dot_nt(jnp.concatenate([q, k], axis=0), k) for q, k in zip(qs, ks)]
    qk = [x[0:RB, :] for x in qkk]
    kk = [x[RB:2 * RB, :] for x in qkk]
    bcol, gcol, tcol, dm, lm, t = [], [], [], [], [], []
    for hh, d in probs:
        col = d * DN_HEADS + h0 + hh
        bcol.append(_col(beta_all, lane, col))
        gcol.append(_col(gf if d == 0 else gb, lane, 2 * DN_HEADS + col))
        tcol.append(_col(tot_all, lane, 2 * DN_HEADS + col))
        gi = jnp.broadcast_to(gcol[-1], (RB, RB))
        dm.append(jnp.where(incl[d], jnp.exp(jnp.minimum(gi - gi.T, 0.0)), 0.0))
        lm.append(jnp.where(r2 != c2, bcol[-1] * kk[hh] * dm[-1], 0.0))
        t.append(eye - jnp.where((r2 // 2) == (c2 // 2), lm[-1], 0.0))
    for mask in level_masks:
        t_op = [ti.astype(MXU_DT) for ti in t]
        tc = [_dot(ti, jnp.where(mask, li, 0.0)) for ti, li in zip(t_op, lm)]
        t = [_dot(eye - tci, ti) for ti, tci in zip(t_op, tc)]
    stores = []
    outs = ((wq_f, u_f, kg_f, qk_f, dl_f), (wq_b, u_b, kg_b, qk_b, dl_b))
    for p, (hh, d) in enumerate(probs):
        eg = jnp.exp(gcol[p])
        sol = _dot(t[p], jnp.concatenate([vs[hh] * bcol[p], ks[hh] * (bcol[p] * eg)], axis=1))
        sol_u = sol[:, 0:HEAD]
        sol_w = sol[:, HEAD:2 * HEAD]
        qkm = jnp.where(incl[d], qk[hh] * dm[p], 0.0)
        q_g = qs[hh] * eg
        k_gt = (ks[hh] * jnp.exp(tcol[p] - gcol[p])).T
        dl = jnp.exp(tcol[p])
        stores.append((outs[d], hh, d, sol_w, q_g, sol_u, k_gt, qkm, dl))
    for (wq_ref, u_ref, kg_ref, qk_ref, dl_ref), hh, d, sol_w, q_g, sol_u, k_gt, qkm, dl in stores:
        for j in range(nchunk):
            slot = j if d == 0 else nchunk - 1 - j
            rows = slice(j * CHUNK, (j + 1) * CHUNK)
            wq_ref[hh, slot, 0:CHUNK, :] = sol_w[rows, :].astype(wq_ref.dtype)
            wq_ref[hh, slot, CHUNK:2 * CHUNK, :] = q_g[rows, :].astype(wq_ref.dtype)
            u_ref[hh, slot, :, :] = sol_u[rows, :]
            kg_ref[hh, slot, :, :] = k_gt[:, rows].astype(kg_ref.dtype)
            qk_ref[hh, slot, :, :] = qkm[rows, rows].astype(qk_ref.dtype)
            dl_ref[hh, slot, :, :] = jnp.broadcast_to(dl[j * CHUNK:j * CHUNK + 1, :], (8, HEAD))


def _dn_prep(p, conv_w, layer, ps, par, *, batch, blocks_per_seq):
    r = p.shape[0]
    nblk = r // RB
    sub = RB // 8
    last8 = r // 8 - 1
    n_lat = batch * blocks_per_seq
    nchunk = RB // CHUNK
    npos = (blocks_per_seq + 1) * nchunk
    chains = batch * DN_HEADS

    def bidx(i):
        return jnp.where(i < n_lat, i // blocks_per_seq, i - n_lat)

    def pos_f(i):
        return jnp.where(i < n_lat, 1 + i % blocks_per_seq, 0)

    def pos_b(i):
        return jnp.where(i < n_lat, blocks_per_seq - i % blocks_per_seq, 0)

    hsteps = DN_HEADS // DN_HPS
    wide = DN_HPS * HEAD

    def ospec(shape, pos):
        return pl.BlockSpec((DN_HPS, nchunk) + shape, lambda i, h: (bidx(i) * hsteps + h, pos(i), 0, 0))

    shapes = [s for s, _ in _dn_scan_operands()]
    out_specs = [ospec(s, pos_f) for s in shapes] + [ospec(s, pos_b) for s in shapes]
    out_shape = [jax.ShapeDtypeStruct((chains, npos) + s, dt) for s, dt in _dn_scan_operands()] * 2
    def conv_specs(part):
        col = lambda h: part * hsteps + h
        return [pl.BlockSpec((8, wide), lambda i, h: (jnp.maximum(i * sub - 1, 0), col(h))),
                pl.BlockSpec((RB, wide), lambda i, h: (i, col(h))),
                pl.BlockSpec((8, wide), lambda i, h: (jnp.minimum((i + 1) * sub, last8), col(h))),
                pl.BlockSpec((None, 8, wide), lambda i, h: (layer, 0, col(h)))]

    return pl.pallas_call(
        functools.partial(_dn_prep_kernel, blocks_per_seq=blocks_per_seq, n_lat_blocks=n_lat),
        grid=(nblk, hsteps),
        in_specs=conv_specs(0) + conv_specs(1) + conv_specs(2) +
                 [pl.BlockSpec((RB, HEAD), lambda i, h: (i, 0)),
                  pl.BlockSpec((8, HEAD), lambda i, h: (0, 0))],
        out_specs=out_specs,
        out_shape=out_shape,
        scratch_shapes=[pltpu.VMEM((3, RB + 16, wide), F32)],
        compiler_params=_cp(("parallel", "parallel")),
        name="dn_prep",
    )(*([p, p, p, conv_w] * 3), ps, par)


def _dn_scan_kernel(wq_f, u_f, kg_f, qk_f, dl_f, wq_b, u_b, kg_b, qk_b, dl_b, o_f, o_b, s_ref):
    @pl.when(pl.program_id(0) == 0)
    def _():
        s_ref[...] = jnp.zeros_like(s_ref)

    ins = ((wq_f, u_f, kg_f, qk_f, dl_f, o_f), (wq_b, u_b, kg_b, qk_b, dl_b, o_b))
    nchain = wq_f.shape[0]
    chains = [(d, c) for d in range(2) for c in range(nchain)]
    state = [s_ref[d * nchain + c] for d, c in chains]
    r = [_dot(ins[d][0][c, 0], s) for (d, c), s in zip(chains, state)]
    v_new = [ins[d][1][c, 0] - ri[0:CHUNK, :] for (d, c), ri in zip(chains, r)]
    out = [ri[CHUNK:2 * CHUNK, :] + _dot(ins[d][3][c, 0], vi) for (d, c), ri, vi in zip(chains, r, v_new)]
    new_state = [s * ins[d][4][c, 0, 0:1, :] + _dot(ins[d][2][c, 0], vi)
                 for (d, c), s, vi in zip(chains, state, v_new)]
    for (d, c), oi, si in zip(chains, out, new_state):
        ins[d][5][c, 0] = oi
        s_ref[d * nchain + c] = si


def _dn_scan(prep):
    chains, npos = prep[0].shape[:2]
    shapes = [s for s, _ in _dn_scan_operands()]
    spec = lambda s: pl.BlockSpec((chains, 1) + s, lambda n: (0, n, 0, 0))
    in_specs = [spec(s) for s in shapes] * 2
    o_sds = jax.ShapeDtypeStruct((chains, npos, CHUNK, HEAD), F32)
    return pl.pallas_call(
        _dn_scan_kernel,
        grid=(npos,),
        in_specs=in_specs,
        out_specs=[spec((CHUNK, HEAD)), spec((CHUNK, HEAD))],
        out_shape=[o_sds, o_sds],
        scratch_shapes=[pltpu.VMEM((2 * chains, HEAD, HEAD), F32)],
        compiler_params=_cp(("arbitrary",)),
        name="dn_scan",
    )(*prep)


def _dn_out_kernel(of_ref, ob_ref, gate_ref, nw_ref, o_ref):
    nchunk = RB // CHUNK
    nw = nw_ref[0:1, :]
    for h in range(DN_HEADS):
        for j in range(nchunk):
            o = of_ref[h, j] + ob_ref[h, nchunk - 1 - j]
            y = o * lax.rsqrt(jnp.mean(o * o, axis=-1, keepdims=True) + EPS) * nw
            gt = gate_ref[j * CHUNK:(j + 1) * CHUNK, h * HEAD:(h + 1) * HEAD]
            o_ref[j * CHUNK:(j + 1) * CHUNK, h * HEAD:(h + 1) * HEAD] = (y * _silu(gt)).astype(o_ref.dtype)


def _dn_out(o_f, o_b, p, norm_w, *, batch, blocks_per_seq, nblk):
    n_lat = batch * blocks_per_seq
    nchunk = RB // CHUNK
    bidx = lambda i: jnp.where(i < n_lat, i // blocks_per_seq, i - n_lat)
    pos_f = lambda i: jnp.where(i < n_lat, 1 + i % blocks_per_seq, 0)
    pos_b = lambda i: jnp.where(i < n_lat, blocks_per_seq - i % blocks_per_seq, 0)
    return pl.pallas_call(
        _dn_out_kernel,
        grid=(nblk,),
        in_specs=[pl.BlockSpec((DN_HEADS, nchunk, CHUNK, HEAD), lambda i: (bidx(i), pos_f(i), 0, 0)),
                  pl.BlockSpec((DN_HEADS, nchunk, CHUNK, HEAD), lambda i: (bidx(i), pos_b(i), 0, 0)),
                  pl.BlockSpec((RB, DN_W), lambda i: (i, C_GATE // DN_W)),
                  pl.BlockSpec((8, HEAD), lambda i: (0, 0))],
        out_specs=pl.BlockSpec((RB, DN_W), lambda i: (i, 0)),
        out_shape=jax.ShapeDtypeStruct((nblk * RB, DN_W), MXU_DT),
        compiler_params=_cp(("parallel",)),
        name="dn_out",
    )(o_f, o_b, p, norm_w)


def _attn_prep_kernel(q_ref, k_ref, v_ref, cos_ref, sin_ref, qw_ref, kw_ref, qo_ref, ko_ref, vo_ref, *, n_lat_blocks):
    is_lat = pl.program_id(0) < n_lat_blocks
    cos = jnp.where(is_lat, cos_ref[...], 1.0)
    sin = jnp.where(is_lat, sin_ref[...], 0.0)
    lane = lax.broadcasted_iota(jnp.int32, (RB, HEAD), 1)
    first_half = (lane % (HEAD // 2)) < (HEAD // 4)

    def norm_rope(x, w):
        y = x * lax.rsqrt(jnp.mean(x * x, axis=-1, keepdims=True) + EPS) * w
        swapped = jnp.where(first_half, pltpu.roll(y, HEAD - HEAD // 4, 1), pltpu.roll(y, HEAD // 4, 1))
        return y * cos + swapped * sin

    q_scale = HEAD ** -0.5 * math.log2(math.e)
    ones_col = jnp.where(lane == 0, 1.0, 0.0).astype(vo_ref.dtype)
    for h in range(AQ_HEADS):
        qh = norm_rope(q_ref[:, h * HEAD:(h + 1) * HEAD], qw_ref[0:1, :]) * q_scale
        qo_ref[0, h] = qh.astype(qo_ref.dtype)
    for h in range(AKV_HEADS):
        ko_ref[0, h] = norm_rope(k_ref[:, h * HEAD:(h + 1) * HEAD], kw_ref[0:1, :]).astype(ko_ref.dtype)
        vo_ref[0, h, :, 0:HEAD] = v_ref[:, h * HEAD:(h + 1) * HEAD].astype(vo_ref.dtype)
        vo_ref[0, h, :, HEAD:2 * HEAD] = ones_col


def _attn_prep(p, cos, sin, qw, kw, *, batch, blocks_per_seq):
    r = p.shape[0]
    nblk = r // RB
    n_lat = batch * blocks_per_seq
    ltot = (blocks_per_seq + 1) * RB
    bidx = lambda i: jnp.where(i < n_lat, i // blocks_per_seq, i - n_lat)
    lblk = lambda i: jnp.where(i < n_lat, i % blocks_per_seq, blocks_per_seq)
    tblk = lambda i: jnp.where(i < n_lat, i % blocks_per_seq, 0)
    return pl.pallas_call(
        functools.partial(_attn_prep_kernel, n_lat_blocks=n_lat),
        grid=(nblk,),
        in_specs=[pl.BlockSpec((RB, AQ_W), lambda i: (i, C_AQ // AQ_W)),
                  pl.BlockSpec((RB, AKV_W), lambda i: (i, C_AK // AKV_W)),
                  pl.BlockSpec((RB, AKV_W), lambda i: (i, C_AV // AKV_W)),
                  pl.BlockSpec((RB, HEAD), lambda i: (tblk(i), 0)),
                  pl.BlockSpec((RB, HEAD), lambda i: (tblk(i), 0)),
                  pl.BlockSpec((8, HEAD), lambda i: (0, 0)),
                  pl.BlockSpec((8, HEAD), lambda i: (0, 0))],
        out_specs=[pl.BlockSpec((1, AQ_HEADS, RB, HEAD), lambda i: (bidx(i), 0, lblk(i), 0)),
                   pl.BlockSpec((1, AKV_HEADS, RB, HEAD), lambda i: (bidx(i), 0, lblk(i), 0)),
                   pl.BlockSpec((1, AKV_HEADS, RB, 2 * HEAD), lambda i: (bidx(i), 0, lblk(i), 0))],
        out_shape=[jax.ShapeDtypeStruct((batch, AQ_HEADS, ltot, HEAD), MXU_DT),
                   jax.ShapeDtypeStruct((batch, AKV_HEADS, ltot, HEAD), MXU_DT),
                   jax.ShapeDtypeStruct((batch, AKV_HEADS, ltot, 2 * HEAD), MXU_DT)],
        compiler_params=_cp(("parallel",)),
        name="attn_prep",
    )(p, p, p, cos, sin, qw, kw)


def _key_chunk(ltot, cap):
    return max(kc for kc in range(HEAD, cap + 1, HEAD) if ltot % kc == 0)


def _attn_kernel(q_ref, k_ref, v_ref, o_ref, *, n_lat_qblocks, lat_len, key_chunk):
    grp = AQ_HEADS // AKV_HEADS

    def attend(start, stop, kc):
        q = q_ref[0].reshape(grp * TQ, HEAD)
        m = None
        acc = None
        for c0 in range(start, stop, kc):
            s = lax.dot_general(q, k_ref[0, 0, c0:c0 + kc, :], (((1,), (1,)), ((), ())), preferred_element_type=F32)
            m_c = jnp.max(s, axis=-1, keepdims=True)
            m_new = m_c if m is None else jnp.maximum(m, m_c)
            pv = jnp.dot(jnp.exp2(s - m_new).astype(MXU_DT), v_ref[0, 0, c0:c0 + kc, :], preferred_element_type=F32)
            acc = pv if m is None else acc * jnp.exp2(m - m_new) + pv
            m = m_new
        o = acc[:, 0:HEAD] / acc[:, HEAD:HEAD + 1]
        for g in range(grp):
            o_ref[:, g * HEAD:(g + 1) * HEAD] = o[g * TQ:(g + 1) * TQ, :].astype(o_ref.dtype)

    qi = pl.program_id(2)
    ltot = k_ref.shape[2]

    @pl.when(qi < n_lat_qblocks)
    def _():
        attend(0, ltot, key_chunk)

    @pl.when(qi >= n_lat_qblocks)
    def _():
        attend(lat_len, ltot, ltot - lat_len)


def _attention(qh, kh, vh, *, lat_len, with_ctx, n_rows):
    batch, _, ltot, _ = qh.shape
    grp = AQ_HEADS // AKV_HEADS
    n_lat_q = lat_len // TQ
    n_q = ltot // TQ if with_ctx else n_lat_q
    n_ctx_q = (ltot - lat_len) // TQ

    def orow(b, qi):
        return jnp.where(qi < n_lat_q, b * n_lat_q + qi, batch * n_lat_q + b * n_ctx_q + (qi - n_lat_q))

    return pl.pallas_call(
        functools.partial(_attn_kernel, n_lat_qblocks=n_lat_q, lat_len=lat_len,
                          key_chunk=_key_chunk(ltot, ATTN_KEY_CHUNK_CAP)),
        grid=(batch, AKV_HEADS, n_q),
        in_specs=[pl.BlockSpec((1, grp, TQ, HEAD), lambda b, g, qi: (b, g, qi, 0)),
                  pl.BlockSpec((1, 1, ltot, HEAD), lambda b, g, qi: (b, g, 0, 0)),
                  pl.BlockSpec((1, 1, ltot, 2 * HEAD), lambda b, g, qi: (b, g, 0, 0))],
        out_specs=pl.BlockSpec((TQ, grp * HEAD), lambda b, g, qi: (orow(b, qi), g)),
        out_shape=jax.ShapeDtypeStruct((n_rows, AQ_W), MXU_DT),
        compiler_params=_cp(("parallel", "parallel", "arbitrary")),
        name="attention",
    )(qh, kh, vh)


def _plain_matmul_kernel(x_ref, w_ref, o_ref):
    o_ref[...] = _dot(x_ref[...], w_ref[...]).astype(o_ref.dtype)


def _fnet_channel_dft(p, cs, *, row_off, n_rows):
    return pl.pallas_call(
        _plain_matmul_kernel,
        grid=(n_rows // TM,),
        in_specs=[pl.BlockSpec((TM, FN_W), lambda i: (row_off // TM + i, C_FN // FN_W)),
                  pl.BlockSpec((FN_W, 2 * FN_W), lambda i: (0, 0))],
        out_specs=pl.BlockSpec((TM, 2 * FN_W), lambda i: (i, 0)),
        out_shape=jax.ShapeDtypeStruct((n_rows, 2 * FN_W), MXU_DT),
        compiler_params=_cp(("parallel",)),
        name="fnet_channel_dft",
    )(p, cs)


def _fnet_pos_kernel(*refs, batch):
    cos_ref, sin_ref = refs[0], refs[1]
    y_refs = refs[2:2 + batch]
    w_ref = refs[2 + batch]
    o_ref = refs[3 + batch]
    acc_ref = refs[4 + batch]
    kk = pl.program_id(1)

    @pl.when(kk == 0)
    def _():
        acc_ref[...] = jnp.zeros_like(acc_ref)

    c = cos_ref[...]
    s = sin_ref[...]
    for b in range(batch):
        y = y_refs[b][...]
        acc_ref[b] += (jnp.dot(c, y[:, 0:FN_W], preferred_element_type=F32)
                       - jnp.dot(s, y[:, FN_W:2 * FN_W], preferred_element_type=F32))

    @pl.when(kk == pl.num_programs(1) - 1)
    def _():
        for b in range(batch):
            o_ref[b] = _dot(acc_ref[b], w_ref[...]).astype(o_ref.dtype)


def _fnet_pos_dft(y, cos_t, sin_t, fnet_w, layer, *, batch, seq_len, row_off):
    t = min(1024, seq_len)
    nt = seq_len // t
    y_specs = [pl.BlockSpec((t, 2 * FN_W), functools.partial(lambda i, k, b: (row_off // t + b * nt + k, 0), b=b))
               for b in range(batch)]
    return pl.pallas_call(
        functools.partial(_fnet_pos_kernel, batch=batch),
        grid=(nt, nt),
        in_specs=[pl.BlockSpec((t, t), lambda i, k: (i, k)),
                  pl.BlockSpec((t, t), lambda i, k: (i, k))] + y_specs +
                 [pl.BlockSpec((None, FN_W, FN_W), lambda i, k: (layer, 0, 0))],
        out_specs=pl.BlockSpec((batch, t, FN_W), lambda i, k: (0, i, 0)),
        out_shape=jax.ShapeDtypeStruct((batch, seq_len, FN_W), MXU_DT),
        scratch_shapes=[pltpu.VMEM((batch, t, FN_W), F32)],
        compiler_params=_cp(("parallel", "arbitrary")),
        name="fnet_pos_dft",
    )(cos_t, sin_t, *([y] * batch), fnet_w)


def _dft_tables(n):
    idx = jnp.arange(n, dtype=jnp.int32)
    ang = ((idx[:, None] * idx[None, :]) % n).astype(F32) * (2.0 * math.pi / n)
    scale = n ** -0.5
    return jnp.cos(ang) * scale, jnp.sin(ang) * scale


def _fft_stage1_kernel(y_ref, c_ref, s_ref, twc_ref, tws_ref, o_ref, *, j1_per_step):
    lane = lax.broadcasted_iota(jnp.int32, (HEAD, HEAD), 1)
    c = c_ref[...]
    s = s_ref[...]
    for t in range(j1_per_step):
        j1 = pl.program_id(1) * j1_per_step + t
        yc = y_ref[:, t * 2 * FN_W:t * 2 * FN_W + FN_W]
        ys = y_ref[:, t * 2 * FN_W + FN_W:(t + 1) * 2 * FN_W]
        ar = jnp.dot(c, yc, preferred_element_type=F32) - jnp.dot(s, ys, preferred_element_type=F32)
        ai = -(jnp.dot(s, yc, preferred_element_type=F32) + jnp.dot(c, ys, preferred_element_type=F32))
        tc = _col(twc_ref[...], lane, j1)
        ts = _col(tws_ref[...], lane, j1)
        o_ref[t, :, 0:FN_W] = (ar * tc + ai * ts).astype(o_ref.dtype)
        o_ref[t, :, FN_W:2 * FN_W] = (ai * tc - ar * ts).astype(o_ref.dtype)


def _fft_stage2_kernel(b_ref, c_ref, s_ref, w_ref, o_ref, *, k2_per_step):
    c = c_ref[...]
    s = s_ref[...]
    for t in range(k2_per_step):
        br = b_ref[:, t * 2 * FN_W:t * 2 * FN_W + FN_W]
        bi = b_ref[:, t * 2 * FN_W + FN_W:(t + 1) * 2 * FN_W]
        xr = jnp.dot(c, br, preferred_element_type=F32) + jnp.dot(s, bi, preferred_element_type=F32)
        o_ref[:, t * FN_W:(t + 1) * FN_W] = _dot(xr, w_ref[...]).astype(o_ref.dtype)


def _fnet_pos_fft(y, tabs, fnet_w, layer, *, batch, seq_len):
    c2, s2, twc, tws, c1, s1 = tabs
    l1 = seq_len // HEAD
    per = min(8, l1)
    yv = y.reshape(y.shape[0] // l1, l1 * 2 * FN_W)
    b_arr = pl.pallas_call(
        functools.partial(_fft_stage1_kernel, j1_per_step=per),
        grid=(batch, l1 // per),
        in_specs=[pl.BlockSpec((HEAD, per * 2 * FN_W), lambda b, j: (b, j)),
                  pl.BlockSpec((HEAD, HEAD), lambda b, j: (0, 0)),
                  pl.BlockSpec((HEAD, HEAD), lambda b, j: (0, 0)),
                  pl.BlockSpec((HEAD, HEAD), lambda b, j: (0, 0)),
                  pl.BlockSpec((HEAD, HEAD), lambda b, j: (0, 0))],
        out_specs=pl.BlockSpec((per, HEAD, 2 * FN_W), lambda b, j: (b * (l1 // per) + j, 0, 0)),
        out_shape=jax.ShapeDtypeStruct((batch * l1, HEAD, 2 * FN_W), MXU_DT),
        compiler_params=_cp(("parallel", "parallel")),
        name="fnet_fft_stage1",
    )(yv, c2, s2, twc, tws)
    bv = b_arr.reshape(batch, l1, HEAD * 2 * FN_W)
    out = pl.pallas_call(
        functools.partial(_fft_stage2_kernel, k2_per_step=8),
        grid=(batch, HEAD // 8),
        in_specs=[pl.BlockSpec((None, l1, 8 * 2 * FN_W), lambda b, k: (b, 0, k)),
                  pl.BlockSpec((l1, l1), lambda b, k: (0, 0)),
                  pl.BlockSpec((l1, l1), lambda b, k: (0, 0)),
                  pl.BlockSpec((None, FN_W, FN_W), lambda b, k: (layer, 0, 0))],
        out_specs=pl.BlockSpec((None, l1, 8 * FN_W), lambda b, k: (b, 0, k)),
        out_shape=jax.ShapeDtypeStruct((batch, l1, HEAD * FN_W), MXU_DT),
        compiler_params=_cp(("parallel", "parallel")),
        name="fnet_fft_stage2",
    )(bv, c1, s1, fnet_w)
    return out.reshape(batch * seq_len, FN_W)


def _fft_tables(seq_len):
    l1 = seq_len // HEAD
    c2, s2 = _dft_tables(HEAD)
    c1, s1 = _dft_tables(l1)
    k2 = jnp.arange(HEAD, dtype=jnp.int32)[:, None]
    j1 = jnp.arange(HEAD, dtype=jnp.int32)[None, :]
    ang = ((k2 * j1) % seq_len).astype(F32) * (2.0 * math.pi / seq_len)
    return (c2.astype(MXU_DT), s2.astype(MXU_DT), jnp.cos(ang), jnp.sin(ang), c1.astype(MXU_DT), s1.astype(MXU_DT))


def _out_proj_kernel(x_ref, dn_ref, at_ref, fn_ref, w_ref, mod_ref, g_ref, b_ref, o_ref, *, alpha):
    acc = jnp.dot(dn_ref[...], w_ref[0:DN_W, :], preferred_element_type=F32)
    acc += jnp.dot(at_ref[...], w_ref[DN_W:DN_W + AQ_W, :], preferred_element_type=F32)
    acc += _dot(fn_ref[...], w_ref[DN_W + AQ_W:, :])
    y = alpha * x_ref[...] + mod_ref[2:3, :] * acc
    o_ref[...] = _ln(y) * g_ref[0:1, :] + b_ref[0:1, :]


def _out_proj(x, a_dn, a_at, a_fn, w_out, mod, g, b, layer, *, seg_of_tile, n_rows, alpha):
    d = x.shape[1]
    tm = RB
    seg = lambda i: seg_of_tile(i // (TM // tm))
    return pl.pallas_call(
        functools.partial(_out_proj_kernel, alpha=alpha),
        grid=(n_rows // tm,),
        in_specs=[pl.BlockSpec((tm, d), lambda i: (i, 0)),
                  pl.BlockSpec((tm, DN_W), lambda i: (i, 0)),
                  pl.BlockSpec((tm, AQ_W), lambda i: (i, 0)),
                  pl.BlockSpec((tm, FN_W), lambda i: (i, 0)),
                  pl.BlockSpec((None, DN_W + AQ_W + FN_W, d), lambda i: (layer, 0, 0)),
                  pl.BlockSpec((None, 6, d), lambda i: (seg(i), 0, 0)),
                  pl.BlockSpec((None, 1, d), lambda i: (layer, 0, 0)),
                  pl.BlockSpec((None, 1, d), lambda i: (layer, 0, 0))],
        out_specs=pl.BlockSpec((tm, d), lambda i: (i, 0)),
        out_shape=jax.ShapeDtypeStruct((n_rows, d), F32),
        compiler_params=_cp(("parallel",)),
        name="out_proj",
    )(x, a_dn, a_at, a_fn, w_out, mod, g, b)


def _ffn_kernel(x_ref, mod_ref, wg_ref, wu_ref, wd_ref, g_ref, b_ref, o_ref, h_ref, acc_ref, *, alpha):
    f = pl.program_id(1)

    @pl.when(f == 0)
    def _():
        h = _ln(x_ref[...]) * (1.0 + mod_ref[4:5, :]) + mod_ref[3:4, :]
        h_ref[...] = h.astype(MXU_DT)
        acc_ref[...] = jnp.zeros_like(acc_ref)

    h = h_ref[...]
    gate = jnp.dot(h, wg_ref[...], preferred_element_type=F32)
    up = jnp.dot(h, wu_ref[...], preferred_element_type=F32)
    acc_ref[...] += jnp.dot((_silu(gate) * up).astype(MXU_DT), wd_ref[...], preferred_element_type=F32)

    @pl.when(f == pl.num_programs(1) - 1)
    def _():
        y = alpha * x_ref[...] + mod_ref[5:6, :] * acc_ref[...]
        o_ref[...] = _ln(y) * g_ref[0:1, :] + b_ref[0:1, :]


def _ffn(x, mod, wg, wu, wd, g, b, layer, li, *, seg_of_tile, n_rows, alpha, tf):
    d = x.shape[1]
    ff = wg.shape[2]
    return pl.pallas_call(
        functools.partial(_ffn_kernel, alpha=alpha),
        grid=(n_rows // TM, ff // tf),
        in_specs=[pl.BlockSpec((TM, d), lambda i, f: (i, 0)),
                  pl.BlockSpec((None, 6, d), lambda i, f: (seg_of_tile(i), 0, 0)),
                  pl.BlockSpec((None, d, tf), lambda i, f: (li, 0, f)),
                  pl.BlockSpec((None, d, tf), lambda i, f: (li, 0, f)),
                  pl.BlockSpec((None, tf, d), lambda i, f: (li, f, 0)),
                  pl.BlockSpec((None, 1, d), lambda i, f: (layer, 0, 0)),
                  pl.BlockSpec((None, 1, d), lambda i, f: (layer, 0, 0))],
        out_specs=pl.BlockSpec((TM, d), lambda i, f: (i, 0)),
        out_shape=jax.ShapeDtypeStruct((n_rows, d), F32),
        scratch_shapes=[pltpu.VMEM((TM, d), MXU_DT), pltpu.VMEM((TM, d), F32)],
        compiler_params=_cp(("parallel", "arbitrary")),
        name="ffn",
    )(x, mod, wg, wu, wd, g, b)


def _router_kernel(x_ref, mod_ref, rw_ref, h_ref, idx_ref, gate_ref):
    h = _ln(x_ref[...]) * (1.0 + mod_ref[4:5, :]) + mod_ref[3:4, :]
    h_ref[...] = h.astype(h_ref.dtype)
    logits = jnp.dot(h, rw_ref[...], precision=HI, preferred_element_type=F32)
    lane = lax.broadcasted_iota(jnp.int32, logits.shape, 1)
    neg = jnp.float32(-jnp.inf)
    l1 = jnp.where(lane < N_EXPERTS, logits, neg)
    m1 = jnp.max(l1, axis=-1, keepdims=True)
    i1 = jnp.min(jnp.where(l1 == m1, lane, HEAD), axis=-1, keepdims=True)
    l2 = jnp.where(lane == i1, neg, l1)
    m2 = jnp.max(l2, axis=-1, keepdims=True)
    i2 = jnp.min(jnp.where(l2 == m2, lane, HEAD), axis=-1, keepdims=True)
    e = jnp.exp(m2 - m1)
    g1 = 1.0 / (1.0 + e)
    g2 = e / (1.0 + e)
    idx_ref[...] = jnp.where(lane == 0, i1, jnp.where(lane == 1, i2, 0))
    gate_ref[...] = jnp.where(lane == 0, g1, jnp.where(lane == 1, g2, 0.0))


def _router(x, mod, rw, *, seg_of_tile, n_rows):
    d = x.shape[1]
    return pl.pallas_call(
        _router_kernel,
        grid=(n_rows // TM,),
        in_specs=[pl.BlockSpec((TM, d), lambda i: (i, 0)),
                  pl.BlockSpec((None, 6, d), lambda i: (seg_of_tile(i), 0, 0)),
                  pl.BlockSpec((d, HEAD), lambda i: (0, 0))],
        out_specs=[pl.BlockSpec((TM, d), lambda i: (i, 0)),
                   pl.BlockSpec((TM, HEAD), lambda i: (i, 0)),
                   pl.BlockSpec((TM, HEAD), lambda i: (i, 0))],
        out_shape=[jax.ShapeDtypeStruct((n_rows, d), MXU_DT),
                   jax.ShapeDtypeStruct((n_rows, HEAD), jnp.int32),
                   jax.ShapeDtypeStruct((n_rows, HEAD), F32)],
        compiler_params=_cp(("parallel",)),
        name="router",
    )(x, mod, rw)


def _moe_kernel(te_ref, nv_ref, h_ref, wg_ref, wu_ref, wd_ref, o_ref, acc_ref):
    t = pl.program_id(0)
    f = pl.program_id(1)
    last = pl.num_programs(1) - 1
    valid = t < nv_ref[0]

    @pl.when(jnp.logical_and(valid, f == 0))
    def _():
        acc_ref[...] = jnp.zeros_like(acc_ref)

    @pl.when(valid)
    def _():
        h = h_ref[...]
        gate = _dot(h, wg_ref[...])
        up = _dot(h, wu_ref[...])
        acc_ref[...] += _dot(_silu(gate) * up, wd_ref[...])

    @pl.when(jnp.logical_and(valid, f == last))
    def _():
        o_ref[...] = acc_ref[...].astype(o_ref.dtype)

    @pl.when(jnp.logical_and(jnp.logical_not(valid), f == last))
    def _():
        o_ref[...] = jnp.zeros_like(o_ref)


def _moe_experts(tile_expert, n_valid, h_sorted, wg, wu, wd, mi, *, tf):
    mp, d = h_sorted.shape
    ff = wg.shape[3]
    nf = ff // tf

    def fidx(t, f, nv):
        return jnp.where(t < nv[0], f, nf - 1)

    grid_spec = pltpu.PrefetchScalarGridSpec(
        num_scalar_prefetch=2,
        grid=(mp // TM_MOE, nf),
        in_specs=[pl.BlockSpec((TM_MOE, d), lambda t, f, te, nv: (t, 0)),
                  pl.BlockSpec((None, None, d, tf), lambda t, f, te, nv: (mi, te[t], 0, fidx(t, f, nv))),
                  pl.BlockSpec((None, None, d, tf), lambda t, f, te, nv: (mi, te[t], 0, fidx(t, f, nv))),
                  pl.BlockSpec((None, None, tf, d), lambda t, f, te, nv: (mi, te[t], fidx(t, f, nv), 0))],
        out_specs=pl.BlockSpec((TM_MOE, d), lambda t, f, te, nv: (t, 0)),
        scratch_shapes=[pltpu.VMEM((TM_MOE, d), F32)],
    )
    return pl.pallas_call(
        _moe_kernel,
        grid_spec=grid_spec,
        out_shape=jax.ShapeDtypeStruct((mp, d), MXU_DT),
        compiler_params=_cp(("arbitrary", "arbitrary")),
        name="moe_experts",
    )(tile_expert, n_valid, h_sorted, wg, wu, wd)


def _moe_combine_kernel(x_ref, y1_ref, y2_ref, gate_ref, mod_ref, g_ref, b_ref, o_ref, *, alpha):
    gates = gate_ref[...]
    y = gates[:, 0:1] * y1_ref[...].astype(F32) + gates[:, 1:2] * y2_ref[...].astype(F32)
    z = alpha * x_ref[...] + mod_ref[5:6, :] * y
    o_ref[...] = _ln(z) * g_ref[0:1, :] + b_ref[0:1, :]


def _moe_combine(x, y12, gates, mod, g, b, layer, *, seg_of_tile, n_rows, alpha):
    d = x.shape[1]
    tm = RB
    nt = n_rows // tm
    seg = lambda i: seg_of_tile(i // (TM // tm))
    return pl.pallas_call(
        functools.partial(_moe_combine_kernel, alpha=alpha),
        grid=(nt,),
        in_specs=[pl.BlockSpec((tm, d), lambda i: (i, 0)),
                  pl.BlockSpec((tm, d), lambda i: (i, 0)),
                  pl.BlockSpec((tm, d), lambda i: (nt + i, 0)),
                  pl.BlockSpec((tm, HEAD), lambda i: (i, 0)),
                  pl.BlockSpec((None, 6, d), lambda i: (seg(i), 0, 0)),
                  pl.BlockSpec((None, 1, d), lambda i: (layer, 0, 0)),
                  pl.BlockSpec((None, 1, d), lambda i: (layer, 0, 0))],
        out_specs=pl.BlockSpec((tm, d), lambda i: (i, 0)),
        out_shape=jax.ShapeDtypeStruct((n_rows, d), F32),
        compiler_params=_cp(("parallel",)),
        name="moe_combine",
    )(x, y12, y12, gates, mod, g, b)


def _moe_layer(x, mod, rw, wg, wu, wd, g, b, layer, mi, *, seg_of_tile, n_rows, alpha, tf):
    h, idx, gates = _router(x, mod, rw, seg_of_tile=seg_of_tile, n_rows=n_rows)
    na = 2 * n_rows
    mp = na + N_EXPERTS * TM_MOE
    e_flat = idx[:, 0:2].reshape(na)
    onehot = (e_flat[:, None] == jnp.arange(N_EXPERTS, dtype=jnp.int32)[None, :]).astype(jnp.int32)
    csum = jnp.cumsum(onehot, axis=0)
    rank = jnp.sum((csum - onehot) * onehot, axis=1)
    counts = csum[-1]
    padded = ((counts + TM_MOE - 1) // TM_MOE) * TM_MOE
    ends = jnp.cumsum(padded)
    dest = (ends - padded)[e_flat] + rank
    src_token = jnp.zeros((mp,), jnp.int32).at[dest].set(jnp.arange(na, dtype=jnp.int32) // 2)
    tile_start = jnp.arange(mp // TM_MOE, dtype=jnp.int32) * TM_MOE
    tile_expert = jnp.sum((ends[None, :] <= tile_start[:, None]).astype(jnp.int32), axis=1)
    tile_expert = jnp.minimum(tile_expert, N_EXPERTS - 1)
    n_valid = (ends[-1] // TM_MOE).astype(jnp.int32).reshape(1)
    h_sorted = jnp.take(h, src_token, axis=0, mode="clip")
    y_sorted = _moe_experts(tile_expert, n_valid, h_sorted, wg, wu, wd, mi, tf=tf)
    y12 = jnp.take(y_sorted, dest.reshape(n_rows, 2).T.reshape(na), axis=0, mode="clip")
    return _moe_combine(x, y12, gates, mod, g, b, layer, seg_of_tile=seg_of_tile, n_rows=n_rows, alpha=alpha)


def _rope_tables(seq_len):
    axis_dim = HEAD // 2
    inv = ROPE_THETA ** (-jnp.arange(0, axis_dim, 2, dtype=F32) / axis_dim)
    t = jnp.arange(seq_len, dtype=jnp.int32)
    row = (t // GRID_W).astype(F32)[:, None] * inv
    col = (t % GRID_W).astype(F32)[:, None] * inv
    cos = jnp.concatenate([jnp.cos(row), jnp.cos(row), jnp.cos(col), jnp.cos(col)], axis=-1)
    sin = jnp.concatenate([-jnp.sin(row), jnp.sin(row), -jnp.sin(col), jnp.sin(col)], axis=-1)
    return cos, sin


def _pad_rows8(v):
    return jnp.zeros((8, v.shape[-1]), F32).at[0].set(v.astype(F32))


def kernel(x, c, ctx, c_ctx, w_mod, b_mod, w_in, dn_conv, dn_a_log, dn_dt_bias, dn_norm, attn_q_norm, attn_k_norm,
           fnet_w, w_out, ln1_g, ln1_b, ln2_g, ln2_b, ffn_w_gate, ffn_w_up, ffn_w_down, router, moe_w_gate,
           moe_w_up, moe_w_down):
    batch, seq, d = x.shape
    ctx_len = ctx.shape[1]
    depth = w_mod.shape[0]
    ff = ffn_w_gate.shape[2]
    assert ctx_len == RB and seq % TM == 0 and (batch * ctx_len) % TM == 0 and TM % RB == 0
    alpha = (2 * depth) ** 0.25
    n_lat = batch * seq
    n_all = n_lat + batch * ctx_len
    tiles_per_seq = seq // TM
    blocks_per_seq = seq // RB
    seg_of_tile = lambda i: jnp.minimum(i // tiles_per_seq, batch)
    tf = 512 if ff % 512 == 0 else ff

    w_in_big = jnp.concatenate([w_in[:, :, :4 * DN_W], w_in[:, :, 4 * DN_W + 4 * DN_HEADS:]], axis=-1).astype(MXU_DT)
    w_in_small = jnp.pad(w_in[:, :, 4 * DN_W:4 * DN_W + 4 * DN_HEADS],
                         ((0, 0), (0, 0), (0, HEAD - 4 * DN_HEADS))).astype(MXU_DT)
    conv_w = jnp.pad(dn_conv, ((0, 0), (0, 8 - CONV_K), (0, 0)))
    w_out_c = w_out.astype(MXU_DT)
    fnet_w_c = fnet_w.astype(MXU_DT)
    ffn_wg, ffn_wu, ffn_wd = ffn_w_gate.astype(MXU_DT), ffn_w_up.astype(MXU_DT), ffn_w_down.astype(MXU_DT)
    router_p = jnp.pad(router, ((0, 0), (0, 0), (0, HEAD - N_EXPERTS)))
    b_mod3 = b_mod[:, None, :]
    ln1_g3, ln1_b3, ln2_g3, ln2_b3 = ln1_g[:, None, :], ln1_b[:, None, :], ln2_g[:, None, :], ln2_b[:, None, :]

    rope_cos, rope_sin = _rope_tables(seq)
    lat_tabs = _fft_tables(seq)
    ctx_cos, ctx_sin = (t.astype(MXU_DT) for t in _dft_tables(ctx_len))
    ch_cos, ch_sin = _dft_tables(HEAD)
    eye_g = jnp.eye(FN_GROUPS, dtype=F32)
    ch_cs = jnp.concatenate([jnp.kron(eye_g, ch_cos), jnp.kron(eye_g, ch_sin)], axis=-1).astype(MXU_DT)

    c8 = jnp.zeros((8, d), F32).at[0:batch].set(c).at[batch].set(c_ctx)
    xs = jnp.concatenate([x.reshape(n_lat, d), ctx.reshape(batch * ctx_len, d)], axis=0)

    for layer in range(depth):
        need_ctx = layer < depth - 1
        rows_out = n_all if need_ctx else n_lat
        mod = _modulation(c8, w_mod, b_mod3, layer).reshape(8, 6, d)

        p = _ln_mod_matmul(xs, mod, w_in_big, layer, seg_of_tile=seg_of_tile, shift_row=0, tn=P_COLS // 2)
        ps = _ln_mod_matmul(xs, mod, w_in_small, layer, seg_of_tile=seg_of_tile, shift_row=0, tn=HEAD)

        par = (jnp.zeros((8, HEAD), F32)
               .at[0, 2 * DN_HEADS:4 * DN_HEADS].set(dn_dt_bias[layer].reshape(-1))
               .at[1, 2 * DN_HEADS:4 * DN_HEADS].set(dn_a_log[layer].reshape(-1)))
        prep = _dn_prep(p, conv_w, layer, ps, par, batch=batch, blocks_per_seq=blocks_per_seq)
        o_f, o_b = _dn_scan(prep)
        a_dn = _dn_out(o_f, o_b, p, _pad_rows8(dn_norm[layer]), batch=batch, blocks_per_seq=blocks_per_seq,
                       nblk=rows_out // RB)

        qh, kh, vh = _attn_prep(p, rope_cos, rope_sin, _pad_rows8(attn_q_norm[layer]), _pad_rows8(attn_k_norm[layer]),
                                batch=batch, blocks_per_seq=blocks_per_seq)
        a_at = _attention(qh, kh, vh, lat_len=seq, with_ctx=need_ctx, n_rows=rows_out)

        y_lat = _fnet_channel_dft(p, ch_cs, row_off=0, n_rows=n_lat)
        a_fn = _fnet_pos_fft(y_lat, lat_tabs, fnet_w_c, layer, batch=batch, seq_len=seq)
        if need_ctx:
            y_ctx = _fnet_channel_dft(p, ch_cs, row_off=n_lat, n_rows=batch * ctx_len)
            a_fn_c = _fnet_pos_dft(y_ctx, ctx_cos, ctx_sin, fnet_w_c, layer, batch=batch, seq_len=ctx_len, row_off=0)
            a_fn = jnp.concatenate([a_fn, a_fn_c.reshape(batch * ctx_len, FN_W)], axis=0)

        xs = _out_proj(xs, a_dn, a_at, a_fn, w_out_c, mod, ln1_g3, ln1_b3, layer, seg_of_tile=seg_of_tile,
                       n_rows=rows_out, alpha=alpha)

        li = layer // 2
        if layer % 2 == 0:
            xs = _ffn(xs, mod, ffn_wg, ffn_wu, ffn_wd, ln2_g3, ln2_b3, layer, li, seg_of_tile=seg_of_tile,
                      n_rows=rows_out, alpha=alpha, tf=tf)
        else:
            xs = _moe_layer(xs, mod, router_p[li], moe_w_gate, moe_w_up, moe_w_down, ln2_g3, ln2_b3, layer, li,
                            seg_of_tile=seg_of_tile, n_rows=rows_out, alpha=alpha,
                            tf=TF_MOE if ff % TF_MOE == 0 else ff)
    return xs[:n_lat].reshape(batch, seq, d)
```

```python
import functools
import math

import jax
import jax.numpy as jnp
from jax import lax
from jax.experimental import pallas as pl
from jax.experimental.pallas import tpu as pltpu

F32 = jnp.float32
MXU_DT = jnp.bfloat16
HI = lax.Precision.HIGHEST

EPS = 1e-6
HEAD = 128
DN_HEADS = 8
AQ_HEADS = 4
AKV_HEADS = 2
FN_GROUPS = 4
CHUNK = 64
CONV_K = 5
GRID_W = 64
ROPE_THETA = 10000.0
N_EXPERTS = 8
DN_W = DN_HEADS * HEAD
AQ_W = AQ_HEADS * HEAD
AKV_W = AKV_HEADS * HEAD
FN_W = FN_GROUPS * HEAD
C_DNQ, C_DNK, C_DNV = 0, DN_W, 2 * DN_W
C_GATE = 3 * DN_W
C_AQ = 4 * DN_W
C_AK = C_AQ + AQ_W
C_AV = C_AK + AKV_W
C_FN = C_AV + AKV_W
P_COLS = C_FN + FN_W

TM = 512
TM_MOE = 768
TF_MOE = 256
RB = 256
DN_HPS = 4
TQ = 256
ATTN_KEY_CHUNK_CAP = 1536
V7X_VMEM_LIMIT = 56 * 1024 * 1024


def _cp(sem, vmem=V7X_VMEM_LIMIT):
    return pltpu.CompilerParams(dimension_semantics=sem, vmem_limit_bytes=vmem)


def _silu(x):
    return x * jax.nn.sigmoid(x)


def _ln(x):
    mu = jnp.mean(x, axis=-1, keepdims=True)
    xc = x - mu
    var = jnp.mean(xc * xc, axis=-1, keepdims=True)
    return xc * lax.rsqrt(var + EPS)


def _dot(a, b):
    return jnp.dot(a.astype(MXU_DT), b.astype(MXU_DT), preferred_element_type=F32)


def _dot_nt(a, b):
    return lax.dot_general(a.astype(MXU_DT), b.astype(MXU_DT), (((1,), (1,)), ((), ())),
                           preferred_element_type=F32)


def _mod_kernel(c_ref, w_ref, b_ref, o_ref):
    o_ref[...] = _dot(_silu(c_ref[...]), w_ref[...]) + b_ref[...]


def _modulation(c8, w_mod, b_mod):
    d = c8.shape[1]
    depth, _, n = w_mod.shape
    tn = d // 2
    return pl.pallas_call(
        _mod_kernel,
        grid=(depth, n // tn),
        in_specs=[pl.BlockSpec((8, d), lambda l, j: (0, 0)),
                  pl.BlockSpec((None, d, tn), lambda l, j: (l, 0, j)),
                  pl.BlockSpec((None, 1, tn), lambda l, j: (l, 0, j))],
        out_specs=pl.BlockSpec((None, 8, tn), lambda l, j: (l, 0, j)),
        out_shape=jax.ShapeDtypeStruct((depth, 8, n), F32),
        compiler_params=_cp(("parallel", "parallel")),
        name="modulation",
    )(c8, w_mod, b_mod)


def _ln_mod_matmul_kernel(x_ref, mod_ref, w_ref, o_ref, h_ref, *, shift_row):
    @pl.when(pl.program_id(1) == 0)
    def _():
        h = _ln(x_ref[...]) * (1.0 + mod_ref[shift_row + 1:shift_row + 2, :]) + mod_ref[shift_row:shift_row + 1, :]
        h_ref[...] = h.astype(MXU_DT)

    o_ref[...] = jnp.dot(h_ref[...], w_ref[...], preferred_element_type=F32)


def _ln_mod_matmul(x, mod, w, layer, *, seg_of_tile, shift_row, tn):
    r, d = x.shape
    n = w.shape[2]
    return pl.pallas_call(
        functools.partial(_ln_mod_matmul_kernel, shift_row=shift_row),
        grid=(r // TM, n // tn),
        in_specs=[pl.BlockSpec((TM, d), lambda i, j: (i, 0)),
                  pl.BlockSpec((None, 6, d), lambda i, j: (seg_of_tile(i), 0, 0)),
                  pl.BlockSpec((None, d, tn), lambda i, j: (layer, 0, j))],
        out_specs=pl.BlockSpec((TM, tn), lambda i, j: (i, j)),
        out_shape=jax.ShapeDtypeStruct((r, n), F32),
        scratch_shapes=[pltpu.VMEM((TM, d), MXU_DT)],
        compiler_params=_cp(("parallel", "arbitrary")),
        name="in_proj",
    )(x, mod, w)


def _short_conv_silu(prev_ref, x_ref, next_ref, w_ref, buf_ref, prev_ok, next_ok):
    buf_ref[0:8, :] = jnp.where(prev_ok, prev_ref[...], 0.0)
    buf_ref[8:8 + RB, :] = x_ref[...]
    buf_ref[8 + RB:16 + RB, :] = jnp.where(next_ok, next_ref[...], 0.0)
    pad = CONV_K // 2
    y = buf_ref[8 - pad:8 - pad + RB, :] * w_ref[0:1, :]
    for tap in range(1, CONV_K):
        y = y + buf_ref[8 - pad + tap:8 - pad + tap + RB, :] * w_ref[tap:tap + 1, :]
    return _silu(y)


def _l2norm(y):
    return y * lax.rsqrt(jnp.sum(y * y, axis=-1, keepdims=True) + EPS)


def _dn_scan_operands():
    return (((2 * CHUNK, HEAD), MXU_DT), ((CHUNK, HEAD), F32), ((HEAD, CHUNK), MXU_DT),
            ((CHUNK, CHUNK), MXU_DT), ((8, HEAD), F32))

def _col(x, lane, idx):
    return jnp.sum(jnp.where(lane == idx, x, 0.0), axis=-1, keepdims=True)


def _dn_prep_kernel(qp_ref, qx_ref, qn_ref, qw_ref, kp_ref, kx_ref, kn_ref, kw_ref, vp_ref, vx_ref, vn_ref, vw_ref,
                    ps_ref, par_ref, wq_f, u_f, kg_f, qk_f, dl_f, wq_b, u_b, kg_b, qk_b, dl_b, buf_ref,
                    *, blocks_per_seq, n_lat_blocks):
    i = pl.program_id(0)
    li = i % blocks_per_seq
    is_lat = i < n_lat_blocks
    prev_ok = jnp.logical_and(is_lat, li > 0)
    next_ok = jnp.logical_and(is_lat, li < blocks_per_seq - 1)
    q_all = _short_conv_silu(qp_ref, qx_ref, qn_ref, qw_ref, buf_ref.at[0], prev_ok, next_ok)
    k_all = _short_conv_silu(kp_ref, kx_ref, kn_ref, kw_ref, buf_ref.at[1], prev_ok, next_ok)
    v_all = _short_conv_silu(vp_ref, vx_ref, vn_ref, vw_ref, buf_ref.at[2], prev_ok, next_ok)
    h0 = pl.program_id(1) * DN_HPS
    nchunk = RB // CHUNK
    ps = ps_ref[...]
    lane = lax.broadcasted_iota(jnp.int32, (RB, HEAD), 1)
    rowc = lax.broadcasted_iota(jnp.int32, (RB, HEAD), 0) % CHUNK
    beta_all = jax.nn.sigmoid(ps)
    z = ps + par_ref[0:1, :]
    dt = jnp.maximum(z, 0.0) + jnp.log(1.0 + jnp.exp(-jnp.abs(z)))
    g_all = -jnp.exp(par_ref[1:2, :]) * dt
    gf = g_all
    gb = g_all
    s = 1
    while s < CHUNK:
        gf = gf + jnp.where(rowc >= s, pltpu.roll(gf, s, 0), 0.0)
        gb = gb + jnp.where(rowc < CHUNK - s, pltpu.roll(gb, RB - s, 0), 0.0)
        s *= 2
    tot_all = gf + gb - g_all

    r2 = lax.broadcasted_iota(jnp.int32, (RB, RB), 0)
    c2 = lax.broadcasted_iota(jnp.int32, (RB, RB), 1)
    eye = (r2 == c2).astype(F32)
    same = (r2 // CHUNK) == (c2 // CHUNK)
    incl = (jnp.logical_and(same, r2 >= c2), jnp.logical_and(same, r2 <= c2))
    level_masks = []
    bs = 2
    while bs < CHUNK:
        level_masks.append(jnp.logical_and((r2 // (2 * bs)) == (c2 // (2 * bs)), (r2 // bs) != (c2 // bs)))
        bs *= 2

    probs = [(hh, d) for hh in range(DN_HPS) for d in range(2)]
    qs = [_l2norm(q_all[:, hh * HEAD:(hh + 1) * HEAD]) * (HEAD ** -0.5) for hh in range(DN_HPS)]
    ks = [_l2norm(k_all[:, hh * HEAD:(hh + 1) * HEAD]) for hh in range(DN_HPS)]
    vs = [v_all[:, hh * HEAD:(hh + 1) * HEAD] for hh in range(DN_HPS)]
    qkk = [_dot_nt(jnp.concatenate([q, k], axis=0), k) for q, k in zip(qs, ks)]
    qk = [x[0:RB, :] for x in qkk]
    kk = [x[RB:2 * RB, :] for x in qkk]
    bcol, gcol, tcol, dm, lm, t = [], [], [], [], [], []
    for hh, d in probs:
        col = d * DN_HEADS + h0 + hh
        bcol.append(_col(beta_all, lane, col))
        gcol.append(_col(gf if d == 0 else gb, lane, 2 * DN_HEADS + col))
        tcol.append(_col(tot_all, lane, 2 * DN_HEADS + col))
        gi = jnp.broadcast_to(gcol[-1], (RB, RB))
        dm.append(jnp.where(incl[d], jnp.exp(jnp.minimum(gi - gi.T, 0.0)), 0.0))
        lm.append(jnp.where(r2 != c2, bcol[-1] * kk[hh] * dm[-1], 0.0))
        t.append(eye - jnp.where((r2 // 2) == (c2 // 2), lm[-1], 0.0))
    for mask in level_masks:
        t_op = [ti.astype(MXU_DT) for ti in t]
        tc = [_dot(ti, jnp.where(mask, li, 0.0)) for ti, li in zip(t_op, lm)]
        t = [_dot(eye - tci, ti) for ti, tci in zip(t_op, tc)]
    stores = []
    outs = ((wq_f, u_f, kg_f, qk_f, dl_f), (wq_b, u_b, kg_b, qk_b, dl_b))
    for p, (hh, d) in enumerate(probs):
        eg = jnp.exp(gcol[p])
        sol = _dot(t[p], jnp.concatenate([vs[hh] * bcol[p], ks[hh] * (bcol[p] * eg)], axis=1))
        sol_u = sol[:, 0:HEAD]
        sol_w = sol[:, HEAD:2 * HEAD]
        qkm = jnp.where(incl[d], qk[hh] * dm[p], 0.0)
        q_g = qs[hh] * eg
        k_gt = (ks[hh] * jnp.exp(tcol[p] - gcol[p])).T
        dl = jnp.exp(tcol[p])
        stores.append((outs[d], hh, d, sol_w, q_g, sol_u, k_gt, qkm, dl))
    for (wq_ref, u_ref, kg_ref, qk_ref, dl_ref), hh, d, sol_w, q_g, sol_u, k_gt, qkm, dl in stores:
        for j in range(nchunk):
            slot = j if d == 0 else nchunk - 1 - j
            rows = slice(j * CHUNK, (j + 1) * CHUNK)
            wq_ref[hh, slot, 0:CHUNK, :] = sol_w[rows, :].astype(wq_ref.dtype)
            wq_ref[hh, slot, CHUNK:2 * CHUNK, :] = q_g[rows, :].astype(wq_ref.dtype)
            u_ref[hh, slot, :, :] = sol_u[rows, :]
            kg_ref[hh, slot, :, :] = k_gt[:, rows].astype(kg_ref.dtype)
            qk_ref[hh, slot, :, :] = qkm[rows, rows].astype(qk_ref.dtype)
            dl_ref[hh, slot, :, :] = jnp.broadcast_to(dl[j * CHUNK:j * CHUNK + 1, :], (8, HEAD))


def _dn_prep(p, conv_w, layer, ps, par, *, batch, blocks_per_seq):
    r = p.shape[0]
    nblk = r // RB
    sub = RB // 8
    last8 = r // 8 - 1
    n_lat = batch * blocks_per_seq
    nchunk = RB // CHUNK
    npos = (blocks_per_seq + 1) * nchunk
    chains = batch * DN_HEADS

    def bidx(i):
        return jnp.where(i < n_lat, i // blocks_per_seq, i - n_lat)

    def pos_f(i):
        return jnp.where(i < n_lat, 1 + i % blocks_per_seq, 0)

    def pos_b(i):
        return jnp.where(i < n_lat, blocks_per_seq - i % blocks_per_seq, 0)

    hsteps = DN_HEADS // DN_HPS
    wide = DN_HPS * HEAD

    def ospec(shape, pos):
        return pl.BlockSpec((DN_HPS, nchunk) + shape, lambda i, h: (bidx(i) * hsteps + h, pos(i), 0, 0))

    shapes = [s for s, _ in _dn_scan_operands()]
    out_specs = [ospec(s, pos_f) for s in shapes] + [ospec(s, pos_b) for s in shapes]
    out_shape = [jax.ShapeDtypeStruct((chains, npos) + s, dt) for s, dt in _dn_scan_operands()] * 2
    def conv_specs(part):
        col = lambda h: part * hsteps + h
        return [pl.BlockSpec((8, wide), lambda i, h: (jnp.maximum(i * sub - 1, 0), col(h))),
                pl.BlockSpec((RB, wide), lambda i, h: (i, col(h))),
                pl.BlockSpec((8, wide), lambda i, h: (jnp.minimum((i + 1) * sub, last8), col(h))),
                pl.BlockSpec((None, 8, wide), lambda i, h: (layer, 0, col(h)))]

    return pl.pallas_call(
        functools.partial(_dn_prep_kernel, blocks_per_seq=blocks_per_seq, n_lat_blocks=n_lat),
        grid=(nblk, hsteps),
        in_specs=conv_specs(0) + conv_specs(1) + conv_specs(2) +
                 [pl.BlockSpec((RB, HEAD), lambda i, h: (i, 0)),
                  pl.BlockSpec((8, HEAD), lambda i, h: (0, 0))],
        out_specs=out_specs,
        out_shape=out_shape,
        scratch_shapes=[pltpu.VMEM((3, RB + 16, wide), F32)],
        compiler_params=_cp(("parallel", "parallel")),
        name="dn_prep",
    )(*([p, p, p, conv_w] * 3), ps, par)


def _dn_scan_kernel(wq_f, u_f, kg_f, qk_f, dl_f, wq_b, u_b, kg_b, qk_b, dl_b, o_f, o_b, s_ref):
    @pl.when(pl.program_id(0) == 0)
    def _():
        s_ref[...] = jnp.zeros_like(s_ref)

    ins = ((wq_f, u_f, kg_f, qk_f, dl_f, o_f), (wq_b, u_b, kg_b, qk_b, dl_b, o_b))
    nchain = wq_f.shape[0]
    chains = [(d, c) for d in range(2) for c in range(nchain)]
    state = [s_ref[d * nchain + c] for d, c in chains]
    r = [_dot(ins[d][0][c, 0], s) for (d, c), s in zip(chains, state)]
    v_new = [ins[d][1][c, 0] - ri[0:CHUNK, :] for (d, c), ri in zip(chains, r)]
    out = [ri[CHUNK:2 * CHUNK, :] + _dot(ins[d][3][c, 0], vi) for (d, c), ri, vi in zip(chains, r, v_new)]
    new_state = [s * ins[d][4][c, 0, 0:1, :] + _dot(ins[d][2][c, 0], vi)
                 for (d, c), s, vi in zip(chains, state, v_new)]
    for (d, c), oi, si in zip(chains, out, new_state):
        ins[d][5][c, 0] = oi
        s_ref[d * nchain + c] = si


def _dn_scan(prep):
    chains, npos = prep[0].shape[:2]
    shapes = [s for s, _ in _dn_scan_operands()]
    spec = lambda s: pl.BlockSpec((chains, 1) + s, lambda n: (0, n, 0, 0))
    in_specs = [spec(s) for s in shapes] * 2
    o_sds = jax.ShapeDtypeStruct((chains, npos, CHUNK, HEAD), F32)
    return pl.pallas_call(
        _dn_scan_kernel,
        grid=(npos,),
        in_specs=in_specs,
        out_specs=[spec((CHUNK, HEAD)), spec((CHUNK, HEAD))],
        out_shape=[o_sds, o_sds],
        scratch_shapes=[pltpu.VMEM((2 * chains, HEAD, HEAD), F32)],
        compiler_params=_cp(("arbitrary",)),
        name="dn_scan",
    )(*prep)


def _dn_gated_norm(of_ref, ob_ref, gate_ref, nw_ref, dst_ref):
    nchunk = RB // CHUNK
    nw = nw_ref[0:1, :]
    for h in range(DN_HEADS):
        for j in range(nchunk):
            o = of_ref[h, j] + ob_ref[h, nchunk - 1 - j]
            y = o * lax.rsqrt(jnp.mean(o * o, axis=-1, keepdims=True) + EPS) * nw
            gt = gate_ref[j * CHUNK:(j + 1) * CHUNK, h * HEAD:(h + 1) * HEAD]
            dst_ref[j * CHUNK:(j + 1) * CHUNK, h * HEAD:(h + 1) * HEAD] = (y * _silu(gt)).astype(dst_ref.dtype)


def _attn_prep_kernel(q_ref, k_ref, v_ref, cos_ref, sin_ref, qw_ref, kw_ref, qo_ref, ko_ref, vo_ref, *, n_lat_blocks):
    is_lat = pl.program_id(0) < n_lat_blocks
    cos = jnp.where(is_lat, cos_ref[...], 1.0)
    sin = jnp.where(is_lat, sin_ref[...], 0.0)
    lane = lax.broadcasted_iota(jnp.int32, (RB, HEAD), 1)
    first_half = (lane % (HEAD // 2)) < (HEAD // 4)

    def norm_rope(x, w):
        y = x * lax.rsqrt(jnp.mean(x * x, axis=-1, keepdims=True) + EPS) * w
        swapped = jnp.where(first_half, pltpu.roll(y, HEAD - HEAD // 4, 1), pltpu.roll(y, HEAD // 4, 1))
        return y * cos + swapped * sin

    q_scale = HEAD ** -0.5 * math.log2(math.e)
    ones_col = jnp.where(lane == 0, 1.0, 0.0).astype(vo_ref.dtype)
    for h in range(AQ_HEADS):
        qh = norm_rope(q_ref[:, h * HEAD:(h + 1) * HEAD], qw_ref[0:1, :]) * q_scale
        qo_ref[0, h] = qh.astype(qo_ref.dtype)
    for h in range(AKV_HEADS):
        ko_ref[0, h] = norm_rope(k_ref[:, h * HEAD:(h + 1) * HEAD], kw_ref[0:1, :]).astype(ko_ref.dtype)
        vo_ref[0, h, :, 0:HEAD] = v_ref[:, h * HEAD:(h + 1) * HEAD].astype(vo_ref.dtype)
        vo_ref[0, h, :, HEAD:2 * HEAD] = ones_col


def _attn_prep(p, cos, sin, qw, kw, *, batch, blocks_per_seq):
    r = p.shape[0]
    nblk = r // RB
    n_lat = batch * blocks_per_seq
    ltot = (blocks_per_seq + 1) * RB
    bidx = lambda i: jnp.where(i < n_lat, i // blocks_per_seq, i - n_lat)
    lblk = lambda i: jnp.where(i < n_lat, i % blocks_per_seq, blocks_per_seq)
    tblk = lambda i: jnp.where(i < n_lat, i % blocks_per_seq, 0)
    return pl.pallas_call(
        functools.partial(_attn_prep_kernel, n_lat_blocks=n_lat),
        grid=(nblk,),
        in_specs=[pl.BlockSpec((RB, AQ_W), lambda i: (i, C_AQ // AQ_W)),
                  pl.BlockSpec((RB, AKV_W), lambda i: (i, C_AK // AKV_W)),
                  pl.BlockSpec((RB, AKV_W), lambda i: (i, C_AV // AKV_W)),
                  pl.BlockSpec((RB, HEAD), lambda i: (tblk(i), 0)),
                  pl.BlockSpec((RB, HEAD), lambda i: (tblk(i), 0)),
                  pl.BlockSpec((8, HEAD), lambda i: (0, 0)),
                  pl.BlockSpec((8, HEAD), lambda i: (0, 0))],
        out_specs=[pl.BlockSpec((1, AQ_HEADS, RB, HEAD), lambda i: (bidx(i), 0, lblk(i), 0)),
                   pl.BlockSpec((1, AKV_HEADS, RB, HEAD), lambda i: (bidx(i), 0, lblk(i), 0)),
                   pl.BlockSpec((1, AKV_HEADS, RB, 2 * HEAD), lambda i: (bidx(i), 0, lblk(i), 0))],
        out_shape=[jax.ShapeDtypeStruct((batch, AQ_HEADS, ltot, HEAD), MXU_DT),
                   jax.ShapeDtypeStruct((batch, AKV_HEADS, ltot, HEAD), MXU_DT),
                   jax.ShapeDtypeStruct((batch, AKV_HEADS, ltot, 2 * HEAD), MXU_DT)],
        compiler_params=_cp(("parallel",)),
        name="attn_prep",
    )(p, p, p, cos, sin, qw, kw)


def _key_chunk(ltot, cap):
    return max(kc for kc in range(HEAD, cap + 1, HEAD) if ltot % kc == 0)


def _attn_kernel(q_ref, k_ref, v_ref, o_ref, *, n_lat_qblocks, lat_len, key_chunk):
    grp = AQ_HEADS // AKV_HEADS

    def attend(start, stop, kc):
        q = q_ref[0].reshape(grp * TQ, HEAD)
        m = None
        acc = None
        for c0 in range(start, stop, kc):
            s = lax.dot_general(q, k_ref[0, 0, c0:c0 + kc, :], (((1,), (1,)), ((), ())), preferred_element_type=F32)
            m_c = jnp.max(s, axis=-1, keepdims=True)
            m_new = m_c if m is None else jnp.maximum(m, m_c)
            pv = jnp.dot(jnp.exp2(s - m_new).astype(MXU_DT), v_ref[0, 0, c0:c0 + kc, :], preferred_element_type=F32)
            acc = pv if m is None else acc * jnp.exp2(m - m_new) + pv
            m = m_new
        o = acc[:, 0:HEAD] / acc[:, HEAD:HEAD + 1]
        for g in range(grp):
            o_ref[:, g * HEAD:(g + 1) * HEAD] = o[g * TQ:(g + 1) * TQ, :].astype(o_ref.dtype)

    qi = pl.program_id(2)
    ltot = k_ref.shape[2]

    @pl.when(qi < n_lat_qblocks)
    def _():
        attend(0, ltot, key_chunk)

    @pl.when(qi >= n_lat_qblocks)
    def _():
        attend(lat_len, ltot, ltot - lat_len)


def _attention(qh, kh, vh, *, lat_len, with_ctx, n_rows):
    batch, _, ltot, _ = qh.shape
    grp = AQ_HEADS // AKV_HEADS
    n_lat_q = lat_len // TQ
    n_q = ltot // TQ if with_ctx else n_lat_q
    n_ctx_q = (ltot - lat_len) // TQ

    def orow(b, qi):
        return jnp.where(qi < n_lat_q, b * n_lat_q + qi, batch * n_lat_q + b * n_ctx_q + (qi - n_lat_q))

    return pl.pallas_call(
        functools.partial(_attn_kernel, n_lat_qblocks=n_lat_q, lat_len=lat_len,
                          key_chunk=_key_chunk(ltot, ATTN_KEY_CHUNK_CAP)),
        grid=(batch, AKV_HEADS, n_q),
        in_specs=[pl.BlockSpec((1, grp, TQ, HEAD), lambda b, g, qi: (b, g, qi, 0)),
                  pl.BlockSpec((1, 1, ltot, HEAD), lambda b, g, qi: (b, g, 0, 0)),
                  pl.BlockSpec((1, 1, ltot, 2 * HEAD), lambda b, g, qi: (b, g, 0, 0))],
        out_specs=pl.BlockSpec((TQ, grp * HEAD), lambda b, g, qi: (orow(b, qi), g)),
        out_shape=jax.ShapeDtypeStruct((n_rows, AQ_W), MXU_DT),
        compiler_params=_cp(("parallel", "parallel", "arbitrary")),
        name="attention",
    )(qh, kh, vh)


def _plain_matmul_kernel(x_ref, w_ref, o_ref):
    o_ref[...] = _dot(x_ref[...], w_ref[...]).astype(o_ref.dtype)


def _fnet_channel_dft(p, cs, *, row_off, n_rows):
    return pl.pallas_call(
        _plain_matmul_kernel,
        grid=(n_rows // TM,),
        in_specs=[pl.BlockSpec((TM, FN_W), lambda i: (row_off // TM + i, C_FN // FN_W)),
                  pl.BlockSpec((FN_W, 2 * FN_W), lambda i: (0, 0))],
        out_specs=pl.BlockSpec((TM, 2 * FN_W), lambda i: (i, 0)),
        out_shape=jax.ShapeDtypeStruct((n_rows, 2 * FN_W), MXU_DT),
        compiler_params=_cp(("parallel",)),
        name="fnet_channel_dft",
    )(p, cs)


def _fnet_pos_kernel(*refs, batch):
    cos_ref, sin_ref = refs[0], refs[1]
    y_refs = refs[2:2 + batch]
    w_ref = refs[2 + batch]
    o_ref = refs[3 + batch]
    acc_ref = refs[4 + batch]
    kk = pl.program_id(1)

    @pl.when(kk == 0)
    def _():
        acc_ref[...] = jnp.zeros_like(acc_ref)

    c = cos_ref[...]
    s = sin_ref[...]
    for b in range(batch):
        y = y_refs[b][...]
        acc_ref[b] += (jnp.dot(c, y[:, 0:FN_W], preferred_element_type=F32)
                       - jnp.dot(s, y[:, FN_W:2 * FN_W], preferred_element_type=F32))

    @pl.when(kk == pl.num_programs(1) - 1)
    def _():
        for b in range(batch):
            o_ref[b] = _dot(acc_ref[b], w_ref[...]).astype(o_ref.dtype)


def _fnet_pos_dft(y, cos_t, sin_t, fnet_w, layer, *, batch, seq_len, row_off):
    t = min(1024, seq_len)
    nt = seq_len // t
    y_specs = [pl.BlockSpec((t, 2 * FN_W), functools.partial(lambda i, k, b: (row_off // t + b * nt + k, 0), b=b))
               for b in range(batch)]
    return pl.pallas_call(
        functools.partial(_fnet_pos_kernel, batch=batch),
        grid=(nt, nt),
        in_specs=[pl.BlockSpec((t, t), lambda i, k: (i, k)),
                  pl.BlockSpec((t, t), lambda i, k: (i, k))] + y_specs +
                 [pl.BlockSpec((None, FN_W, FN_W), lambda i, k: (layer, 0, 0))],
        out_specs=pl.BlockSpec((batch, t, FN_W), lambda i, k: (0, i, 0)),
        out_shape=jax.ShapeDtypeStruct((batch, seq_len, FN_W), MXU_DT),
        scratch_shapes=[pltpu.VMEM((batch, t, FN_W), F32)],
        compiler_params=_cp(("parallel", "arbitrary")),
        name="fnet_pos_dft",
    )(cos_t, sin_t, *([y] * batch), fnet_w)


def _dft_tables(n):
    idx = jnp.arange(n, dtype=jnp.int32)
    ang = ((idx[:, None] * idx[None, :]) % n).astype(F32) * (2.0 * math.pi / n)
    scale = n ** -0.5
    return jnp.cos(ang) * scale, jnp.sin(ang) * scale


def _fft_stage1_kernel(y_ref, c_ref, s_ref, twc_ref, tws_ref, o_ref, *, j1_per_step):
    lane = lax.broadcasted_iota(jnp.int32, (HEAD, HEAD), 1)
    c = c_ref[...]
    s = s_ref[...]
    for t in range(j1_per_step):
        j1 = pl.program_id(1) * j1_per_step + t
        yc = y_ref[:, t * 2 * FN_W:t * 2 * FN_W + FN_W]
        ys = y_ref[:, t * 2 * FN_W + FN_W:(t + 1) * 2 * FN_W]
        ar = jnp.dot(c, yc, preferred_element_type=F32) - jnp.dot(s, ys, preferred_element_type=F32)
        ai = -(jnp.dot(s, yc, preferred_element_type=F32) + jnp.dot(c, ys, preferred_element_type=F32))
        tc = _col(twc_ref[...], lane, j1)
        ts = _col(tws_ref[...], lane, j1)
        o_ref[t, :, 0:FN_W] = (ar * tc + ai * ts).astype(o_ref.dtype)
        o_ref[t, :, FN_W:2 * FN_W] = (ai * tc - ar * ts).astype(o_ref.dtype)


def _fft_stage2_kernel(b_ref, c_ref, s_ref, w_ref, o_ref, *, k2_per_step):
    c = c_ref[...]
    s = s_ref[...]
    for t in range(k2_per_step):
        br = b_ref[:, t * 2 * FN_W:t * 2 * FN_W + FN_W]
        bi = b_ref[:, t * 2 * FN_W + FN_W:(t + 1) * 2 * FN_W]
        xr = jnp.dot(c, br, preferred_element_type=F32) + jnp.dot(s, bi, preferred_element_type=F32)
        o_ref[:, t * FN_W:(t + 1) * FN_W] = _dot(xr, w_ref[...]).astype(o_ref.dtype)


def _fnet_pos_fft(y, tabs, fnet_w, layer, *, batch, seq_len):
    c2, s2, twc, tws, c1, s1 = tabs
    l1 = seq_len // HEAD
    per = min(8, l1)
    yv = y.reshape(y.shape[0] // l1, l1 * 2 * FN_W)
    b_arr = pl.pallas_call(
        functools.partial(_fft_stage1_kernel, j1_per_step=per),
        grid=(batch, l1 // per),
        in_specs=[pl.BlockSpec((HEAD, per * 2 * FN_W), lambda b, j: (b, j)),
                  pl.BlockSpec((HEAD, HEAD), lambda b, j: (0, 0)),
                  pl.BlockSpec((HEAD, HEAD), lambda b, j: (0, 0)),
                  pl.BlockSpec((HEAD, HEAD), lambda b, j: (0, 0)),
                  pl.BlockSpec((HEAD, HEAD), lambda b, j: (0, 0))],
        out_specs=pl.BlockSpec((per, HEAD, 2 * FN_W), lambda b, j: (b * (l1 // per) + j, 0, 0)),
        out_shape=jax.ShapeDtypeStruct((batch * l1, HEAD, 2 * FN_W), MXU_DT),
        compiler_params=_cp(("parallel", "parallel")),
        name="fnet_fft_stage1",
    )(yv, c2, s2, twc, tws)
    bv = b_arr.reshape(batch, l1, HEAD * 2 * FN_W)
    out = pl.pallas_call(
        functools.partial(_fft_stage2_kernel, k2_per_step=8),
        grid=(batch, HEAD // 8),
        in_specs=[pl.BlockSpec((None, l1, 8 * 2 * FN_W), lambda b, k: (b, 0, k)),
                  pl.BlockSpec((l1, l1), lambda b, k: (0, 0)),
                  pl.BlockSpec((l1, l1), lambda b, k: (0, 0)),
                  pl.BlockSpec((None, FN_W, FN_W), lambda b, k: (layer, 0, 0))],
        out_specs=pl.BlockSpec((None, l1, 8 * FN_W), lambda b, k: (b, 0, k)),
        out_shape=jax.ShapeDtypeStruct((batch, l1, HEAD * FN_W), MXU_DT),
        compiler_params=_cp(("parallel", "parallel")),
        name="fnet_fft_stage2",
    )(bv, c1, s1, fnet_w)
    return out.reshape(batch * seq_len, FN_W)


def _fft_tables(seq_len):
    l1 = seq_len // HEAD
    c2, s2 = _dft_tables(HEAD)
    c1, s1 = _dft_tables(l1)
    k2 = jnp.arange(HEAD, dtype=jnp.int32)[:, None]
    j1 = jnp.arange(HEAD, dtype=jnp.int32)[None, :]
    ang = ((k2 * j1) % seq_len).astype(F32) * (2.0 * math.pi / seq_len)
    return (c2.astype(MXU_DT), s2.astype(MXU_DT), jnp.cos(ang), jnp.sin(ang), c1.astype(MXU_DT), s1.astype(MXU_DT))


def _out_proj_kernel(x_ref, of_ref, ob_ref, gate_ref, nw_ref, at_ref, fn_ref, w_ref, mod_ref, g_ref, b_ref, o_ref,
                     dn_ref, *, alpha):
    acc = jnp.dot(at_ref[...], w_ref[DN_W:DN_W + AQ_W, :], preferred_element_type=F32)
    acc += _dot(fn_ref[...], w_ref[DN_W + AQ_W:, :])
    _dn_gated_norm(of_ref, ob_ref, gate_ref, nw_ref, dn_ref)
    acc += jnp.dot(dn_ref[...], w_ref[0:DN_W, :], preferred_element_type=F32)
    y = alpha * x_ref[...] + mod_ref[2:3, :] * acc
    o_ref[...] = _ln(y) * g_ref[0:1, :] + b_ref[0:1, :]


def _out_proj(x, o_f, o_b, p, norm_w, a_at, a_fn, w_out, mod, g, b, layer, *, seg_of_tile, n_rows, alpha, batch,
              blocks_per_seq):
    d = x.shape[1]
    tm = RB
    seg = lambda i: seg_of_tile(i // (TM // tm))
    n_lat = batch * blocks_per_seq
    nchunk = RB // CHUNK
    bidx = lambda i: jnp.where(i < n_lat, i // blocks_per_seq, i - n_lat)
    pos_f = lambda i: jnp.where(i < n_lat, 1 + i % blocks_per_seq, 0)
    pos_b = lambda i: jnp.where(i < n_lat, blocks_per_seq - i % blocks_per_seq, 0)
    return pl.pallas_call(
        functools.partial(_out_proj_kernel, alpha=alpha),
        grid=(n_rows // tm,),
        in_specs=[pl.BlockSpec((tm, d), lambda i: (i, 0)),
                  pl.BlockSpec((DN_HEADS, nchunk, CHUNK, HEAD), lambda i: (bidx(i), pos_f(i), 0, 0)),
                  pl.BlockSpec((DN_HEADS, nchunk, CHUNK, HEAD), lambda i: (bidx(i), pos_b(i), 0, 0)),
                  pl.BlockSpec((RB, DN_W), lambda i: (i, C_GATE // DN_W)),
                  pl.BlockSpec((8, HEAD), lambda i: (0, 0)),
                  pl.BlockSpec((tm, AQ_W), lambda i: (i, 0)),
                  pl.BlockSpec((tm, FN_W), lambda i: (i, 0)),
                  pl.BlockSpec((None, DN_W + AQ_W + FN_W, d), lambda i: (layer, 0, 0)),
                  pl.BlockSpec((None, 6, d), lambda i: (seg(i), 0, 0)),
                  pl.BlockSpec((None, 1, d), lambda i: (layer, 0, 0)),
                  pl.BlockSpec((None, 1, d), lambda i: (layer, 0, 0))],
        out_specs=pl.BlockSpec((tm, d), lambda i: (i, 0)),
        out_shape=jax.ShapeDtypeStruct((n_rows, d), F32),
        scratch_shapes=[pltpu.VMEM((RB, DN_W), MXU_DT)],
        compiler_params=_cp(("parallel",)),
        name="out_proj",
    )(x, o_f, o_b, p, norm_w, a_at, a_fn, w_out, mod, g, b)


def _ffn_kernel(x_ref, mod_ref, wg_ref, wu_ref, wd_ref, g_ref, b_ref, o_ref, h_ref, acc_ref, *, alpha):
    f = pl.program_id(1)

    @pl.when(f == 0)
    def _():
        h = _ln(x_ref[...]) * (1.0 + mod_ref[4:5, :]) + mod_ref[3:4, :]
        h_ref[...] = h.astype(MXU_DT)
        acc_ref[...] = jnp.zeros_like(acc_ref)

    h = h_ref[...]
    gate = jnp.dot(h, wg_ref[...], preferred_element_type=F32)
    up = jnp.dot(h, wu_ref[...], preferred_element_type=F32)
    acc_ref[...] += jnp.dot((_silu(gate) * up).astype(MXU_DT), wd_ref[...], preferred_element_type=F32)

    @pl.when(f == pl.num_programs(1) - 1)
    def _():
        y = alpha * x_ref[...] + mod_ref[5:6, :] * acc_ref[...]
        o_ref[...] = _ln(y) * g_ref[0:1, :] + b_ref[0:1, :]


def _ffn(x, mod, wg, wu, wd, g, b, layer, li, *, seg_of_tile, n_rows, alpha, tf):
    d = x.shape[1]
    ff = wg.shape[2]
    return pl.pallas_call(
        functools.partial(_ffn_kernel, alpha=alpha),
        grid=(n_rows // TM, ff // tf),
        in_specs=[pl.BlockSpec((TM, d), lambda i, f: (i, 0)),
                  pl.BlockSpec((None, 6, d), lambda i, f: (seg_of_tile(i), 0, 0)),
                  pl.BlockSpec((None, d, tf), lambda i, f: (li, 0, f)),
                  pl.BlockSpec((None, d, tf), lambda i, f: (li, 0, f)),
                  pl.BlockSpec((None, tf, d), lambda i, f: (li, f, 0)),
                  pl.BlockSpec((None, 1, d), lambda i, f: (layer, 0, 0)),
                  pl.BlockSpec((None, 1, d), lambda i, f: (layer, 0, 0))],
        out_specs=pl.BlockSpec((TM, d), lambda i, f: (i, 0)),
        out_shape=jax.ShapeDtypeStruct((n_rows, d), F32),
        scratch_shapes=[pltpu.VMEM((TM, d), MXU_DT), pltpu.VMEM((TM, d), F32)],
        compiler_params=_cp(("parallel", "arbitrary")),
        name="ffn",
    )(x, mod, wg, wu, wd, g, b)


def _router_kernel(x_ref, mod_ref, rw_ref, h_ref, idx_ref, gate_ref):
    h = _ln(x_ref[...]) * (1.0 + mod_ref[4:5, :]) + mod_ref[3:4, :]
    h_ref[...] = h.astype(h_ref.dtype)
    logits = jnp.dot(h, rw_ref[...], precision=HI, preferred_element_type=F32)
    lane = lax.broadcasted_iota(jnp.int32, logits.shape, 1)
    neg = jnp.float32(-jnp.inf)
    l1 = jnp.where(lane < N_EXPERTS, logits, neg)
    m1 = jnp.max(l1, axis=-1, keepdims=True)
    i1 = jnp.min(jnp.where(l1 == m1, lane, HEAD), axis=-1, keepdims=True)
    l2 = jnp.where(lane == i1, neg, l1)
    m2 = jnp.max(l2, axis=-1, keepdims=True)
    i2 = jnp.min(jnp.where(l2 == m2, lane, HEAD), axis=-1, keepdims=True)
    e = jnp.exp(m2 - m1)
    g1 = 1.0 / (1.0 + e)
    g2 = e / (1.0 + e)
    idx_ref[...] = jnp.where(lane == 0, i1, jnp.where(lane == 1, i2, 0))
    gate_ref[...] = jnp.where(lane == 0, g1, jnp.where(lane == 1, g2, 0.0))


def _router(x, mod, rw, *, seg_of_tile, n_rows):
    d = x.shape[1]
    return pl.pallas_call(
        _router_kernel,
        grid=(n_rows // TM,),
        in_specs=[pl.BlockSpec((TM, d), lambda i: (i, 0)),
                  pl.BlockSpec((None, 6, d), lambda i: (seg_of_tile(i), 0, 0)),
                  pl.BlockSpec((d, HEAD), lambda i: (0, 0))],
        out_specs=[pl.BlockSpec((TM, d), lambda i: (i, 0)),
                   pl.BlockSpec((TM, HEAD), lambda i: (i, 0)),
                   pl.BlockSpec((TM, HEAD), lambda i: (i, 0))],
        out_shape=[jax.ShapeDtypeStruct((n_rows, d), MXU_DT),
                   jax.ShapeDtypeStruct((n_rows, HEAD), jnp.int32),
                   jax.ShapeDtypeStruct((n_rows, HEAD), F32)],
        compiler_params=_cp(("parallel",)),
        name="router",
    )(x, mod, rw)


def _moe_kernel(te_ref, nv_ref, h_ref, wg_ref, wu_ref, wd_ref, o_ref, acc_ref):
    t = pl.program_id(0)
    f = pl.program_id(1)
    last = pl.num_programs(1) - 1
    valid = t < nv_ref[0]

    @pl.when(jnp.logical_and(valid, f == 0))
    def _():
        acc_ref[...] = jnp.zeros_like(acc_ref)

    @pl.when(valid)
    def _():
        h = h_ref[...]
        gate = _dot(h, wg_ref[...])
        up = _dot(h, wu_ref[...])
        acc_ref[...] += _dot(_silu(gate) * up, wd_ref[...])

    @pl.when(jnp.logical_and(valid, f == last))
    def _():
        o_ref[...] = acc_ref[...].astype(o_ref.dtype)

    @pl.when(jnp.logical_and(jnp.logical_not(valid), f == last))
    def _():
        o_ref[...] = jnp.zeros_like(o_ref)


def _moe_experts(tile_expert, n_valid, h_sorted, wg, wu, wd, mi, *, tf):
    mp, d = h_sorted.shape
    ff = wg.shape[3]
    nf = ff // tf

    def fidx(t, f, nv):
        return jnp.where(t < nv[0], f, nf - 1)

    grid_spec = pltpu.PrefetchScalarGridSpec(
        num_scalar_prefetch=2,
        grid=(mp // TM_MOE, nf),
        in_specs=[pl.BlockSpec((TM_MOE, d), lambda t, f, te, nv: (t, 0)),
                  pl.BlockSpec((None, None, d, tf), lambda t, f, te, nv: (mi, te[t], 0, fidx(t, f, nv))),
                  pl.BlockSpec((None, None, d, tf), lambda t, f, te, nv: (mi, te[t], 0, fidx(t, f, nv))),
                  pl.BlockSpec((None, None, tf, d), lambda t, f, te, nv: (mi, te[t], fidx(t, f, nv), 0))],
        out_specs=pl.BlockSpec((TM_MOE, d), lambda t, f, te, nv: (t, 0)),
        scratch_shapes=[pltpu.VMEM((TM_MOE, d), F32)],
    )
    return pl.pallas_call(
        _moe_kernel,
        grid_spec=grid_spec,
        out_shape=jax.ShapeDtypeStruct((mp, d), MXU_DT),
        compiler_params=_cp(("arbitrary", "arbitrary")),
        name="moe_experts",
    )(tile_expert, n_valid, h_sorted, wg, wu, wd)


def _moe_combine_kernel(x_ref, y1_ref, y2_ref, gate_ref, mod_ref, g_ref, b_ref, o_ref, *, alpha):
    gates = gate_ref[...]
    y = gates[:, 0:1] * y1_ref[...].astype(F32) + gates[:, 1:2] * y2_ref[...].astype(F32)
    z = alpha * x_ref[...] + mod_ref[5:6, :] * y
    o_ref[...] = _ln(z) * g_ref[0:1, :] + b_ref[0:1, :]


def _moe_combine(x, y12, gates, mod, g, b, layer, *, seg_of_tile, n_rows, alpha):
    d = x.shape[1]
    tm = RB
    nt = n_rows // tm
    seg = lambda i: seg_of_tile(i // (TM // tm))
    return pl.pallas_call(
        functools.partial(_moe_combine_kernel, alpha=alpha),
        grid=(nt,),
        in_specs=[pl.BlockSpec((tm, d), lambda i: (i, 0)),
                  pl.BlockSpec((tm, d), lambda i: (i, 0)),
                  pl.BlockSpec((tm, d), lambda i: (nt + i, 0)),
                  pl.BlockSpec((tm, HEAD), lambda i: (i, 0)),
                  pl.BlockSpec((None, 6, d), lambda i: (seg(i), 0, 0)),
                  pl.BlockSpec((None, 1, d), lambda i: (layer, 0, 0)),
                  pl.BlockSpec((None, 1, d), lambda i: (layer, 0, 0))],
        out_specs=pl.BlockSpec((tm, d), lambda i: (i, 0)),
        out_shape=jax.ShapeDtypeStruct((n_rows, d), F32),
        compiler_params=_cp(("parallel",)),
        name="moe_combine",
    )(x, y12, y12, gates, mod, g, b)


def _moe_layer(x, mod, rw, wg, wu, wd, g, b, layer, mi, *, seg_of_tile, n_rows, alpha, tf):
    h, idx, gates = _router(x, mod, rw, seg_of_tile=seg_of_tile, n_rows=n_rows)
    na = 2 * n_rows
    mp = na + N_EXPERTS * TM_MOE
    e_flat = idx[:, 0:2].reshape(na)
    onehot = (e_flat[:, None] == jnp.arange(N_EXPERTS, dtype=jnp.int32)[None, :]).astype(jnp.int32)
    csum = jnp.cumsum(onehot, axis=0)
    rank = jnp.sum((csum - onehot) * onehot, axis=1)
    counts = csum[-1]
    padded = ((counts + TM_MOE - 1) // TM_MOE) * TM_MOE
    ends = jnp.cumsum(padded)
    dest = (ends - padded)[e_flat] + rank
    src_token = (jnp.arange(mp, dtype=jnp.int32) % n_rows).at[dest].set(jnp.arange(na, dtype=jnp.int32) // 2)
    tile_start = jnp.arange(mp // TM_MOE, dtype=jnp.int32) * TM_MOE
    tile_expert = jnp.sum((ends[None, :] <= tile_start[:, None]).astype(jnp.int32), axis=1)
    tile_expert = jnp.minimum(tile_expert, N_EXPERTS - 1)
    n_valid = (ends[-1] // TM_MOE).astype(jnp.int32).reshape(1)
    h_sorted = jnp.take(h, src_token, axis=0, mode="clip")
    y_sorted = _moe_experts(tile_expert, n_valid, h_sorted, wg, wu, wd, mi, tf=tf)
    y12 = jnp.take(y_sorted, dest.reshape(n_rows, 2).T.reshape(na), axis=0, mode="clip")
    return _moe_combine(x, y12, gates, mod, g, b, layer, seg_of_tile=seg_of_tile, n_rows=n_rows, alpha=alpha)


def _rope_tables(seq_len):
    axis_dim = HEAD // 2
    inv = ROPE_THETA ** (-jnp.arange(0, axis_dim, 2, dtype=F32) / axis_dim)
    t = jnp.arange(seq_len, dtype=jnp.int32)
    row = (t // GRID_W).astype(F32)[:, None] * inv
    col = (t % GRID_W).astype(F32)[:, None] * inv
    cos = jnp.concatenate([jnp.cos(row), jnp.cos(row), jnp.cos(col), jnp.cos(col)], axis=-1)
    sin = jnp.concatenate([-jnp.sin(row), jnp.sin(row), -jnp.sin(col), jnp.sin(col)], axis=-1)
    return cos, sin


def _pad_rows8(v):
    return jnp.zeros((8, v.shape[-1]), F32).at[0].set(v.astype(F32))


def kernel(x, c, ctx, c_ctx, w_mod, b_mod, w_in, dn_conv, dn_a_log, dn_dt_bias, dn_norm, attn_q_norm, attn_k_norm,
           fnet_w, w_out, ln1_g, ln1_b, ln2_g, ln2_b, ffn_w_gate, ffn_w_up, ffn_w_down, router, moe_w_gate,
           moe_w_up, moe_w_down):
    batch, seq, d = x.shape
    ctx_len = ctx.shape[1]
    depth = w_mod.shape[0]
    ff = ffn_w_gate.shape[2]
    assert ctx_len == RB and seq % TM == 0 and (batch * ctx_len) % TM == 0 and TM % RB == 0
    alpha = (2 * depth) ** 0.25
    n_lat = batch * seq
    n_all = n_lat + batch * ctx_len
    tiles_per_seq = seq // TM
    blocks_per_seq = seq // RB
    seg_of_tile = lambda i: jnp.minimum(i // tiles_per_seq, batch)
    tf = 512 if ff % 512 == 0 else ff

    w_in_big = jnp.concatenate([w_in[:, :, :4 * DN_W], w_in[:, :, 4 * DN_W + 4 * DN_HEADS:]], axis=-1).astype(MXU_DT)
    w_in_small = jnp.pad(w_in[:, :, 4 * DN_W:4 * DN_W + 4 * DN_HEADS],
                         ((0, 0), (0, 0), (0, HEAD - 4 * DN_HEADS))).astype(MXU_DT)
    conv_w = jnp.pad(dn_conv, ((0, 0), (0, 8 - CONV_K), (0, 0)))
    w_out_c = w_out.astype(MXU_DT)
    fnet_w_c = fnet_w.astype(MXU_DT)
    ffn_wg, ffn_wu, ffn_wd = ffn_w_gate.astype(MXU_DT), ffn_w_up.astype(MXU_DT), ffn_w_down.astype(MXU_DT)
    router_p = jnp.pad(router, ((0, 0), (0, 0), (0, HEAD - N_EXPERTS)))
    b_mod3 = b_mod[:, None, :]
    ln1_g3, ln1_b3, ln2_g3, ln2_b3 = ln1_g[:, None, :], ln1_b[:, None, :], ln2_g[:, None, :], ln2_b[:, None, :]

    rope_cos, rope_sin = _rope_tables(seq)
    lat_tabs = _fft_tables(seq)
    ctx_cos, ctx_sin = (t.astype(MXU_DT) for t in _dft_tables(ctx_len))
    ch_cos, ch_sin = _dft_tables(HEAD)
    eye_g = jnp.eye(FN_GROUPS, dtype=F32)
    ch_cs = jnp.concatenate([jnp.kron(eye_g, ch_cos), jnp.kron(eye_g, ch_sin)], axis=-1).astype(MXU_DT)

    c8 = jnp.zeros((8, d), F32).at[0:batch].set(c).at[batch].set(c_ctx)
    xs = jnp.concatenate([x.reshape(n_lat, d), ctx.reshape(batch * ctx_len, d)], axis=0)

    mod_all = _modulation(c8, w_mod, b_mod3).reshape(depth, 8, 6, d)

    for layer in range(depth):
        need_ctx = layer < depth - 1
        rows_out = n_all if need_ctx else n_lat
        mod = mod_all[layer]

        p = _ln_mod_matmul(xs, mod, w_in_big, layer, seg_of_tile=seg_of_tile, shift_row=0, tn=P_COLS // 2)
        ps = _ln_mod_matmul(xs, mod, w_in_small, layer, seg_of_tile=seg_of_tile, shift_row=0, tn=HEAD)

        par = (jnp.zeros((8, HEAD), F32)
               .at[0, 2 * DN_HEADS:4 * DN_HEADS].set(dn_dt_bias[layer].reshape(-1))
               .at[1, 2 * DN_HEADS:4 * DN_HEADS].set(dn_a_log[layer].reshape(-1)))
        prep = _dn_prep(p, conv_w, layer, ps, par, batch=batch, blocks_per_seq=blocks_per_seq)
        o_f, o_b = _dn_scan(prep)

        qh, kh, vh = _attn_prep(p, rope_cos, rope_sin, _pad_rows8(attn_q_norm[layer]), _pad_rows8(attn_k_norm[layer]),
                                batch=batch, blocks_per_seq=blocks_per_seq)
        a_at = _attention(qh, kh, vh, lat_len=seq, with_ctx=need_ctx, n_rows=rows_out)

        y_lat = _fnet_channel_dft(p, ch_cs, row_off=0, n_rows=n_lat)
        a_fn = _fnet_pos_fft(y_lat, lat_tabs, fnet_w_c, layer, batch=batch, seq_len=seq)
        if need_ctx:
            y_ctx = _fnet_channel_dft(p, ch_cs, row_off=n_lat, n_rows=batch * ctx_len)
            a_fn_c = _fnet_pos_dft(y_ctx, ctx_cos, ctx_sin, fnet_w_c, layer, batch=batch, seq_len=ctx_len, row_off=0)
            a_fn = jnp.concatenate([a_fn, a_fn_c.reshape(batch * ctx_len, FN_W)], axis=0)

        xs = _out_proj(xs, o_f, o_b, p, _pad_rows8(dn_norm[layer]), a_at, a_fn, w_out_c, mod, ln1_g3, ln1_b3, layer,
                       seg_of_tile=seg_of_tile, n_rows=rows_out, alpha=alpha, batch=batch,
                       blocks_per_seq=blocks_per_seq)

        li = layer // 2
        if layer % 2 == 0:
            xs = _ffn(xs, mod, ffn_wg, ffn_wu, ffn_wd, ln2_g3, ln2_b3, layer, li, seg_of_tile=seg_of_tile,
                      n_rows=rows_out, alpha=alpha, tf=tf)
        else:
            xs = _moe_layer(xs, mod, router_p[li], moe_w_gate, moe_w_up, moe_w_down, ln2_g3, ln2_b3, layer, li,
                            seg_of_tile=seg_of_tile, n_rows=rows_out, alpha=alpha,
                            tf=TF_MOE if ff % TF_MOE == 0 else ff)
    return xs[:n_lat].reshape(batch, seq, d)
```

```python
import functools
import math

import jax
import jax.numpy as jnp
from jax import lax
from jax.experimental import pallas as pl
from jax.experimental.pallas import tpu as pltpu

F32 = jnp.float32
MXU_DT = jnp.bfloat16
HI = lax.Precision.HIGHEST

EPS = 1e-6
HEAD = 128
DN_HEADS = 8
AQ_HEADS = 4
AKV_HEADS = 2
FN_GROUPS = 4
CHUNK = 64
CONV_K = 5
GRID_W = 64
ROPE_THETA = 10000.0
N_EXPERTS = 8
DN_W = DN_HEADS * HEAD
AQ_W = AQ_HEADS * HEAD
AKV_W = AKV_HEADS * HEAD
FN_W = FN_GROUPS * HEAD
C_DNQ, C_DNK, C_DNV = 0, DN_W, 2 * DN_W
C_GATE = 3 * DN_W
C_AQ = 4 * DN_W
C_AK = C_AQ + AQ_W
C_AV = C_AK + AKV_W
C_FN = C_AV + AKV_W
P_COLS = C_FN + FN_W

TM = 512
TM_MOE = 768
TF_MOE = 512
RB = 256
DN_HPS = 4
TQ = 256
ATTN_KEY_CHUNK_CAP = 1536
V7X_VMEM_LIMIT = 56 * 1024 * 1024
V7X_VMEM_LIMIT_MOE = 60 * 1024 * 1024


def _cp(sem, vmem=V7X_VMEM_LIMIT):
    return pltpu.CompilerParams(dimension_semantics=sem, vmem_limit_bytes=vmem)


def _silu(x):
    return x * jax.nn.sigmoid(x)


def _ln(x):
    mu = jnp.mean(x, axis=-1, keepdims=True)
    xc = x - mu
    var = jnp.mean(xc * xc, axis=-1, keepdims=True)
    return xc * lax.rsqrt(var + EPS)


def _dot(a, b):
    return jnp.dot(a.astype(MXU_DT), b.astype(MXU_DT), preferred_element_type=F32)


def _dot_nt(a, b):
    return lax.dot_general(a.astype(MXU_DT), b.astype(MXU_DT), (((1,), (1,)), ((), ())),
                           preferred_element_type=F32)


def _mod_kernel(c_ref, w_ref, b_ref, o_ref):
    o_ref[...] = _dot(_silu(c_ref[...]), w_ref[...]) + b_ref[...]


def _modulation(c8, w_mod, b_mod):
    d = c8.shape[1]
    depth, _, n = w_mod.shape
    tn = d // 2
    return pl.pallas_call(
        _mod_kernel,
        grid=(depth, n // tn),
        in_specs=[pl.BlockSpec((8, d), lambda l, j: (0, 0)),
                  pl.BlockSpec((None, d, tn), lambda l, j: (l, 0, j)),
                  pl.BlockSpec((None, 1, tn), lambda l, j: (l, 0, j))],
        out_specs=pl.BlockSpec((None, 8, tn), lambda l, j: (l, 0, j)),
        out_shape=jax.ShapeDtypeStruct((depth, 8, n), F32),
        compiler_params=_cp(("parallel", "parallel")),
        name="modulation",
    )(c8, w_mod, b_mod)


def _ln_mod_matmul_kernel(x_ref, mod_ref, w_ref, o_ref, h_ref, *, shift_row):
    @pl.when(pl.program_id(1) == 0)
    def _():
        h = _ln(x_ref[...]) * (1.0 + mod_ref[shift_row + 1:shift_row + 2, :]) + mod_ref[shift_row:shift_row + 1, :]
        h_ref[...] = h.astype(MXU_DT)

    o_ref[...] = jnp.dot(h_ref[...], w_ref[...], preferred_element_type=F32)


def _ln_mod_matmul(x, mod, w, layer, *, seg_of_tile, shift_row, tn):
    r, d = x.shape
    n = w.shape[2]
    return pl.pallas_call(
        functools.partial(_ln_mod_matmul_kernel, shift_row=shift_row),
        grid=(r // TM, n // tn),
        in_specs=[pl.BlockSpec((TM, d), lambda i, j: (i, 0)),
                  pl.BlockSpec((None, 6, d), lambda i, j: (seg_of_tile(i), 0, 0)),
                  pl.BlockSpec((None, d, tn), lambda i, j: (layer, 0, j))],
        out_specs=pl.BlockSpec((TM, tn), lambda i, j: (i, j)),
        out_shape=jax.ShapeDtypeStruct((r, n), F32),
        scratch_shapes=[pltpu.VMEM((TM, d), MXU_DT)],
        compiler_params=_cp(("parallel", "arbitrary")),
        name="in_proj",
    )(x, mod, w)


def _short_conv_silu(prev_ref, x_ref, next_ref, w_ref, buf_ref, prev_ok, next_ok):
    buf_ref[0:8, :] = jnp.where(prev_ok, prev_ref[...], 0.0)
    buf_ref[8:8 + RB, :] = x_ref[...]
    buf_ref[8 + RB:16 + RB, :] = jnp.where(next_ok, next_ref[...], 0.0)
    pad = CONV_K // 2
    y = buf_ref[8 - pad:8 - pad + RB, :] * w_ref[0:1, :]
    for tap in range(1, CONV_K):
        y = y + buf_ref[8 - pad + tap:8 - pad + tap + RB, :] * w_ref[tap:tap + 1, :]
    return _silu(y)


def _l2norm(y):
    return y * lax.rsqrt(jnp.sum(y * y, axis=-1, keepdims=True) + EPS)


def _dn_scan_operands():
    return (((2 * CHUNK, HEAD), MXU_DT), ((CHUNK, HEAD), F32), ((HEAD, CHUNK), MXU_DT),
            ((CHUNK, CHUNK), MXU_DT), ((8, HEAD), F32))

def _col(x, lane, idx):
    return jnp.sum(jnp.where(lane == idx, x, 0.0), axis=-1, keepdims=True)


def _dn_prep_kernel(qp_ref, qx_ref, qn_ref, qw_ref, kp_ref, kx_ref, kn_ref, kw_ref, vp_ref, vx_ref, vn_ref, vw_ref,
                    ps_ref, par_ref, wq_f, u_f, kg_f, qk_f, dl_f, wq_b, u_b, kg_b, qk_b, dl_b, buf_ref,
                    *, blocks_per_seq, n_lat_blocks):
    i = pl.program_id(0)
    li = i % blocks_per_seq
    is_lat = i < n_lat_blocks
    prev_ok = jnp.logical_and(is_lat, li > 0)
    next_ok = jnp.logical_and(is_lat, li < blocks_per_seq - 1)
    q_all = _short_conv_silu(qp_ref, qx_ref, qn_ref, qw_ref, buf_ref.at[0], prev_ok, next_ok)
    k_all = _short_conv_silu(kp_ref, kx_ref, kn_ref, kw_ref, buf_ref.at[1], prev_ok, next_ok)
    v_all = _short_conv_silu(vp_ref, vx_ref, vn_ref, vw_ref, buf_ref.at[2], prev_ok, next_ok)
    h0 = pl.program_id(1) * DN_HPS
    nchunk = RB // CHUNK
    ps = ps_ref[...]
    lane = lax.broadcasted_iota(jnp.int32, (RB, HEAD), 1)
    rowc = lax.broadcasted_iota(jnp.int32, (RB, HEAD), 0) % CHUNK
    beta_all = jax.nn.sigmoid(ps)
    z = ps + par_ref[0:1, :]
    dt = jnp.maximum(z, 0.0) + jnp.log(1.0 + jnp.exp(-jnp.abs(z)))
    g_all = -jnp.exp(par_ref[1:2, :]) * dt
    gf = g_all
    gb = g_all
    s = 1
    while s < CHUNK:
        gf = gf + jnp.where(rowc >= s, pltpu.roll(gf, s, 0), 0.0)
        gb = gb + jnp.where(rowc < CHUNK - s, pltpu.roll(gb, RB - s, 0), 0.0)
        s *= 2
    tot_all = gf + gb - g_all

    r2 = lax.broadcasted_iota(jnp.int32, (RB, RB), 0)
    c2 = lax.broadcasted_iota(jnp.int32, (RB, RB), 1)
    eye = (r2 == c2).astype(F32)
    same = (r2 // CHUNK) == (c2 // CHUNK)
    incl = (jnp.logical_and(same, r2 >= c2), jnp.logical_and(same, r2 <= c2))
    level_masks = []
    bs = 2
    while bs < CHUNK:
        level_masks.append(jnp.logical_and((r2 // (2 * bs)) == (c2 // (2 * bs)), (r2 // bs) != (c2 // bs)))
        bs *= 2

    probs = [(hh, d) for hh in range(DN_HPS) for d in range(2)]
    qs = [_l2norm(q_all[:, hh * HEAD:(hh + 1) * HEAD]) * (HEAD ** -0.5) for hh in range(DN_HPS)]
    ks = [_l2norm(k_all[:, hh * HEAD:(hh + 1) * HEAD]) for hh in range(DN_HPS)]
    vs = [v_all[:, hh * HEAD:(hh + 1) * HEAD] for hh in range(DN_HPS)]
    qkk = [_dot_nt(jnp.concatenate([q, k], axis=0), k) for q, k in zip(qs, ks)]
    qk = [x[0:RB, :] for x in qkk]
    kk = [x[RB:2 * RB, :] for x in qkk]
    bcol, gcol, tcol, dm, lm, t = [], [], [], [], [], []
    for hh, d in probs:
        col = d * DN_HEADS + h0 + hh
        bcol.append(_col(beta_all, lane, col))
        gcol.append(_col(gf if d == 0 else gb, lane, 2 * DN_HEADS + col))
        tcol.append(_col(tot_all, lane, 2 * DN_HEADS + col))
        gi = jnp.broadcast_to(gcol[-1], (RB, RB))
        dm.append(jnp.where(incl[d], jnp.exp(jnp.minimum(gi - gi.T, 0.0)), 0.0))
        lm.append(jnp.where(r2 != c2, bcol[-1] * kk[hh] * dm[-1], 0.0))
        t.append(eye - jnp.where((r2 // 2) == (c2 // 2), lm[-1], 0.0))
    for mask in level_masks:
        t_op = [ti.astype(MXU_DT) for ti in t]
        tc = [_dot(ti, jnp.where(mask, li, 0.0)) for ti, li in zip(t_op, lm)]
        t = [_dot(eye - tci, ti) for ti, tci in zip(t_op, tc)]
    stores = []
    outs = ((wq_f, u_f, kg_f, qk_f, dl_f), (wq_b, u_b, kg_b, qk_b, dl_b))
    for p, (hh, d) in enumerate(probs):
        eg = jnp.exp(gcol[p])
        sol = _dot(t[p], jnp.concatenate([vs[hh] * bcol[p], ks[hh] * (bcol[p] * eg)], axis=1))
        sol_u = sol[:, 0:HEAD]
        sol_w = sol[:, HEAD:2 * HEAD]
        qkm = jnp.where(incl[d], qk[hh] * dm[p], 0.0)
        q_g = qs[hh] * eg
        k_gt = (ks[hh] * jnp.exp(tcol[p] - gcol[p])).T
        dl = jnp.exp(tcol[p])
        stores.append((outs[d], hh, d, sol_w, q_g, sol_u, k_gt, qkm, dl))
    for (wq_ref, u_ref, kg_ref, qk_ref, dl_ref), hh, d, sol_w, q_g, sol_u, k_gt, qkm, dl in stores:
        for j in range(nchunk):
            slot = j if d == 0 else nchunk - 1 - j
            rows = slice(j * CHUNK, (j + 1) * CHUNK)
            wq_ref[hh, slot, 0:CHUNK, :] = sol_w[rows, :].astype(wq_ref.dtype)
            wq_ref[hh, slot, CHUNK:2 * CHUNK, :] = q_g[rows, :].astype(wq_ref.dtype)
            u_ref[hh, slot, :, :] = sol_u[rows, :]
            kg_ref[hh, slot, :, :] = k_gt[:, rows].astype(kg_ref.dtype)
            qk_ref[hh, slot, :, :] = qkm[rows, rows].astype(qk_ref.dtype)
            dl_ref[hh, slot, :, :] = jnp.broadcast_to(dl[j * CHUNK:j * CHUNK + 1, :], (8, HEAD))


def _dn_prep(p, conv_w, layer, par, *, batch, blocks_per_seq):
    r = p.shape[0]
    nblk = r // RB
    sub = RB // 8
    last8 = r // 8 - 1
    n_lat = batch * blocks_per_seq
    nchunk = RB // CHUNK
    npos = (blocks_per_seq + 1) * nchunk
    chains = batch * DN_HEADS

    def bidx(i):
        return jnp.where(i < n_lat, i // blocks_per_seq, i - n_lat)

    def pos_f(i):
        return jnp.where(i < n_lat, 1 + i % blocks_per_seq, 0)

    def pos_b(i):
        return jnp.where(i < n_lat, blocks_per_seq - i % blocks_per_seq, 0)

    hsteps = DN_HEADS // DN_HPS
    wide = DN_HPS * HEAD

    def ospec(shape, pos):
        return pl.BlockSpec((DN_HPS, nchunk) + shape, lambda i, h: (bidx(i) * hsteps + h, pos(i), 0, 0))

    shapes = [s for s, _ in _dn_scan_operands()]
    out_specs = [ospec(s, pos_f) for s in shapes] + [ospec(s, pos_b) for s in shapes]
    out_shape = [jax.ShapeDtypeStruct((chains, npos) + s, dt) for s, dt in _dn_scan_operands()] * 2
    def conv_specs(part):
        col = lambda h: part * hsteps + h
        return [pl.BlockSpec((8, wide), lambda i, h: (jnp.maximum(i * sub - 1, 0), col(h))),
                pl.BlockSpec((RB, wide), lambda i, h: (i, col(h))),
                pl.BlockSpec((8, wide), lambda i, h: (jnp.minimum((i + 1) * sub, last8), col(h))),
                pl.BlockSpec((None, 8, wide), lambda i, h: (layer, 0, col(h)))]

    return pl.pallas_call(
        functools.partial(_dn_prep_kernel, blocks_per_seq=blocks_per_seq, n_lat_blocks=n_lat),
        grid=(nblk, hsteps),
        in_specs=conv_specs(0) + conv_specs(1) + conv_specs(2) +
                 [pl.BlockSpec((RB, HEAD), lambda i, h: (i, P_COLS // HEAD)),
                  pl.BlockSpec((8, HEAD), lambda i, h: (0, 0))],
        out_specs=out_specs,
        out_shape=out_shape,
        scratch_shapes=[pltpu.VMEM((3, RB + 16, wide), F32)],
        compiler_params=_cp(("parallel", "parallel")),
        name="dn_prep",
    )(*([p, p, p, conv_w] * 3), p, par)


def _dn_scan_kernel(wq_f, u_f, kg_f, qk_f, dl_f, wq_b, u_b, kg_b, qk_b, dl_b, o_f, o_b, s_ref):
    @pl.when(pl.program_id(0) == 0)
    def _():
        s_ref[...] = jnp.zeros_like(s_ref)

    ins = ((wq_f, u_f, kg_f, qk_f, dl_f, o_f), (wq_b, u_b, kg_b, qk_b, dl_b, o_b))
    nchain = wq_f.shape[0]
    chains = [(d, c) for d in range(2) for c in range(nchain)]
    state = [s_ref[d * nchain + c] for d, c in chains]
    r = [_dot(ins[d][0][c, 0], s) for (d, c), s in zip(chains, state)]
    v_new = [ins[d][1][c, 0] - ri[0:CHUNK, :] for (d, c), ri in zip(chains, r)]
    out = [ri[CHUNK:2 * CHUNK, :] + _dot(ins[d][3][c, 0], vi) for (d, c), ri, vi in zip(chains, r, v_new)]
    new_state = [s * ins[d][4][c, 0, 0:1, :] + _dot(ins[d][2][c, 0], vi)
                 for (d, c), s, vi in zip(chains, state, v_new)]
    for (d, c), oi, si in zip(chains, out, new_state):
        ins[d][5][c, 0] = oi
        s_ref[d * nchain + c] = si


def _dn_scan(prep):
    chains, npos = prep[0].shape[:2]
    shapes = [s for s, _ in _dn_scan_operands()]
    spec = lambda s: pl.BlockSpec((chains, 1) + s, lambda n: (0, n, 0, 0))
    in_specs = [spec(s) for s in shapes] * 2
    o_sds = jax.ShapeDtypeStruct((chains, npos, CHUNK, HEAD), F32)
    return pl.pallas_call(
        _dn_scan_kernel,
        grid=(npos,),
        in_specs=in_specs,
        out_specs=[spec((CHUNK, HEAD)), spec((CHUNK, HEAD))],
        out_shape=[o_sds, o_sds],
        scratch_shapes=[pltpu.VMEM((2 * chains, HEAD, HEAD), F32)],
        compiler_params=_cp(("arbitrary",)),
        name="dn_scan",
    )(*prep)


def _dn_gated_norm(of_ref, ob_ref, gate_ref, nw_ref, dst_ref):
    nchunk = RB // CHUNK
    nw = nw_ref[0:1, :]
    for h in range(DN_HEADS):
        for j in range(nchunk):
            o = of_ref[h, j] + ob_ref[h, nchunk - 1 - j]
            y = o * lax.rsqrt(jnp.mean(o * o, axis=-1, keepdims=True) + EPS) * nw
            gt = gate_ref[j * CHUNK:(j + 1) * CHUNK, h * HEAD:(h + 1) * HEAD]
            dst_ref[j * CHUNK:(j + 1) * CHUNK, h * HEAD:(h + 1) * HEAD] = (y * _silu(gt)).astype(dst_ref.dtype)


def _attn_prep_kernel(q_ref, k_ref, v_ref, cos_ref, sin_ref, qw_ref, kw_ref, qo_ref, ko_ref, vo_ref, *, n_lat_blocks):
    is_lat = pl.program_id(0) < n_lat_blocks
    cos = jnp.where(is_lat, cos_ref[...], 1.0)
    sin = jnp.where(is_lat, sin_ref[...], 0.0)
    lane = lax.broadcasted_iota(jnp.int32, (RB, HEAD), 1)
    first_half = (lane % (HEAD // 2)) < (HEAD // 4)

    def norm_rope(x, w):
        y = x * lax.rsqrt(jnp.mean(x * x, axis=-1, keepdims=True) + EPS) * w
        swapped = jnp.where(first_half, pltpu.roll(y, HEAD - HEAD // 4, 1), pltpu.roll(y, HEAD // 4, 1))
        return y * cos + swapped * sin

    q_scale = HEAD ** -0.5 * math.log2(math.e)
    ones_col = jnp.where(lane == 0, 1.0, 0.0).astype(vo_ref.dtype)
    for h in range(AQ_HEADS):
        qh = norm_rope(q_ref[:, h * HEAD:(h + 1) * HEAD], qw_ref[0:1, :]) * q_scale
        qo_ref[0, h] = qh.astype(qo_ref.dtype)
    for h in range(AKV_HEADS):
        ko_ref[0, h] = norm_rope(k_ref[:, h * HEAD:(h + 1) * HEAD], kw_ref[0:1, :]).astype(ko_ref.dtype)
        vo_ref[0, h, :, 0:HEAD] = v_ref[:, h * HEAD:(h + 1) * HEAD].astype(vo_ref.dtype)
        vo_ref[0, h, :, HEAD:2 * HEAD] = ones_col


def _attn_prep(p, cos, sin, qw, kw, *, batch, blocks_per_seq):
    r = p.shape[0]
    nblk = r // RB
    n_lat = batch * blocks_per_seq
    ltot = (blocks_per_seq + 1) * RB
    bidx = lambda i: jnp.where(i < n_lat, i // blocks_per_seq, i - n_lat)
    lblk = lambda i: jnp.where(i < n_lat, i % blocks_per_seq, blocks_per_seq)
    tblk = lambda i: jnp.where(i < n_lat, i % blocks_per_seq, 0)
    return pl.pallas_call(
        functools.partial(_attn_prep_kernel, n_lat_blocks=n_lat),
        grid=(nblk,),
        in_specs=[pl.BlockSpec((RB, AQ_W), lambda i: (i, C_AQ // AQ_W)),
                  pl.BlockSpec((RB, AKV_W), lambda i: (i, C_AK // AKV_W)),
                  pl.BlockSpec((RB, AKV_W), lambda i: (i, C_AV // AKV_W)),
                  pl.BlockSpec((RB, HEAD), lambda i: (tblk(i), 0)),
                  pl.BlockSpec((RB, HEAD), lambda i: (tblk(i), 0)),
                  pl.BlockSpec((8, HEAD), lambda i: (0, 0)),
                  pl.BlockSpec((8, HEAD), lambda i: (0, 0))],
        out_specs=[pl.BlockSpec((1, AQ_HEADS, RB, HEAD), lambda i: (bidx(i), 0, lblk(i), 0)),
                   pl.BlockSpec((1, AKV_HEADS, RB, HEAD), lambda i: (bidx(i), 0, lblk(i), 0)),
                   pl.BlockSpec((1, AKV_HEADS, RB, 2 * HEAD), lambda i: (bidx(i), 0, lblk(i), 0))],
        out_shape=[jax.ShapeDtypeStruct((batch, AQ_HEADS, ltot, HEAD), MXU_DT),
                   jax.ShapeDtypeStruct((batch, AKV_HEADS, ltot, HEAD), MXU_DT),
                   jax.ShapeDtypeStruct((batch, AKV_HEADS, ltot, 2 * HEAD), MXU_DT)],
        compiler_params=_cp(("parallel",)),
        name="attn_prep",
    )(p, p, p, cos, sin, qw, kw)


def _key_chunk(ltot, cap):
    return max(kc for kc in range(HEAD, cap + 1, HEAD) if ltot % kc == 0)


def _attn_kernel(q_ref, k_ref, v_ref, o_ref, *, n_lat_qblocks, lat_len, key_chunk):
    grp = AQ_HEADS // AKV_HEADS

    def attend(start, stop, kc):
        q = q_ref[0].reshape(grp * TQ, HEAD)
        m = None
        acc = None
        for c0 in range(start, stop, kc):
            s = lax.dot_general(q, k_ref[0, 0, c0:c0 + kc, :], (((1,), (1,)), ((), ())), preferred_element_type=F32)
            m_c = jnp.max(s, axis=-1, keepdims=True)
            m_new = m_c if m is None else jnp.maximum(m, m_c)
            pv = jnp.dot(jnp.exp2(s - m_new).astype(MXU_DT), v_ref[0, 0, c0:c0 + kc, :], preferred_element_type=F32)
            acc = pv if m is None else acc * jnp.exp2(m - m_new) + pv
            m = m_new
        o = acc[:, 0:HEAD] / acc[:, HEAD:HEAD + 1]
        for g in range(grp):
            o_ref[:, g * HEAD:(g + 1) * HEAD] = o[g * TQ:(g + 1) * TQ, :].astype(o_ref.dtype)

    qi = pl.program_id(2)
    ltot = k_ref.shape[2]

    @pl.when(qi < n_lat_qblocks)
    def _():
        attend(0, ltot, key_chunk)

    @pl.when(qi >= n_lat_qblocks)
    def _():
        attend(lat_len, ltot, ltot - lat_len)


def _attention(qh, kh, vh, *, lat_len, with_ctx, n_rows):
    batch, _, ltot, _ = qh.shape
    grp = AQ_HEADS // AKV_HEADS
    n_lat_q = lat_len // TQ
    n_q = ltot // TQ if with_ctx else n_lat_q
    n_ctx_q = (ltot - lat_len) // TQ

    def orow(b, qi):
        return jnp.where(qi < n_lat_q, b * n_lat_q + qi, batch * n_lat_q + b * n_ctx_q + (qi - n_lat_q))

    return pl.pallas_call(
        functools.partial(_attn_kernel, n_lat_qblocks=n_lat_q, lat_len=lat_len,
                          key_chunk=_key_chunk(ltot, ATTN_KEY_CHUNK_CAP)),
        grid=(batch, AKV_HEADS, n_q),
        in_specs=[pl.BlockSpec((1, grp, TQ, HEAD), lambda b, g, qi: (b, g, qi, 0)),
                  pl.BlockSpec((1, 1, ltot, HEAD), lambda b, g, qi: (b, g, 0, 0)),
                  pl.BlockSpec((1, 1, ltot, 2 * HEAD), lambda b, g, qi: (b, g, 0, 0))],
        out_specs=pl.BlockSpec((TQ, grp * HEAD), lambda b, g, qi: (orow(b, qi), g)),
        out_shape=jax.ShapeDtypeStruct((n_rows, AQ_W), MXU_DT),
        compiler_params=_cp(("parallel", "parallel", "arbitrary")),
        name="attention",
    )(qh, kh, vh)


def _plain_matmul_kernel(x_ref, w_ref, o_ref):
    o_ref[...] = _dot(x_ref[...], w_ref[...]).astype(o_ref.dtype)


def _fnet_channel_dft(p, cs, *, row_off, n_rows):
    return pl.pallas_call(
        _plain_matmul_kernel,
        grid=(n_rows // TM,),
        in_specs=[pl.BlockSpec((TM, FN_W), lambda i: (row_off // TM + i, C_FN // FN_W)),
                  pl.BlockSpec((FN_W, 2 * FN_W), lambda i: (0, 0))],
        out_specs=pl.BlockSpec((TM, 2 * FN_W), lambda i: (i, 0)),
        out_shape=jax.ShapeDtypeStruct((n_rows, 2 * FN_W), MXU_DT),
        compiler_params=_cp(("parallel",)),
        name="fnet_channel_dft",
    )(p, cs)


def _fnet_pos_kernel(*refs, batch):
    cos_ref, sin_ref = refs[0], refs[1]
    y_refs = refs[2:2 + batch]
    w_ref = refs[2 + batch]
    o_ref = refs[3 + batch]
    acc_ref = refs[4 + batch]
    kk = pl.program_id(1)

    @pl.when(kk == 0)
    def _():
        acc_ref[...] = jnp.zeros_like(acc_ref)

    c = cos_ref[...]
    s = sin_ref[...]
    for b in range(batch):
        y = y_refs[b][...]
        acc_ref[b] += (jnp.dot(c, y[:, 0:FN_W], preferred_element_type=F32)
                       - jnp.dot(s, y[:, FN_W:2 * FN_W], preferred_element_type=F32))

    @pl.when(kk == pl.num_programs(1) - 1)
    def _():
        for b in range(batch):
            o_ref[b] = _dot(acc_ref[b], w_ref[...]).astype(o_ref.dtype)


def _fnet_pos_dft(y, cos_t, sin_t, fnet_w, layer, *, batch, seq_len, row_off):
    t = min(1024, seq_len)
    nt = seq_len // t
    y_specs = [pl.BlockSpec((t, 2 * FN_W), functools.partial(lambda i, k, b: (row_off // t + b * nt + k, 0), b=b))
               for b in range(batch)]
    return pl.pallas_call(
        functools.partial(_fnet_pos_kernel, batch=batch),
        grid=(nt, nt),
        in_specs=[pl.BlockSpec((t, t), lambda i, k: (i, k)),
                  pl.BlockSpec((t, t), lambda i, k: (i, k))] + y_specs +
                 [pl.BlockSpec((None, FN_W, FN_W), lambda i, k: (layer, 0, 0))],
        out_specs=pl.BlockSpec((batch, t, FN_W), lambda i, k: (0, i, 0)),
        out_shape=jax.ShapeDtypeStruct((batch, seq_len, FN_W), MXU_DT),
        scratch_shapes=[pltpu.VMEM((batch, t, FN_W), F32)],
        compiler_params=_cp(("parallel", "arbitrary")),
        name="fnet_pos_dft",
    )(cos_t, sin_t, *([y] * batch), fnet_w)


def _dft_tables(n):
    idx = jnp.arange(n, dtype=jnp.int32)
    ang = ((idx[:, None] * idx[None, :]) % n).astype(F32) * (2.0 * math.pi / n)
    scale = n ** -0.5
    return jnp.cos(ang) * scale, jnp.sin(ang) * scale


def _fft_stage1_kernel(y_ref, c_ref, s_ref, twc_ref, tws_ref, o_ref, *, j1_per_step):
    lane = lax.broadcasted_iota(jnp.int32, (HEAD, HEAD), 1)
    c = c_ref[...]
    s = s_ref[...]
    for t in range(j1_per_step):
        j1 = pl.program_id(1) * j1_per_step + t
        yc = y_ref[:, t * 2 * FN_W:t * 2 * FN_W + FN_W]
        ys = y_ref[:, t * 2 * FN_W + FN_W:(t + 1) * 2 * FN_W]
        ar = jnp.dot(c, yc, preferred_element_type=F32) - jnp.dot(s, ys, preferred_element_type=F32)
        ai = -(jnp.dot(s, yc, preferred_element_type=F32) + jnp.dot(c, ys, preferred_element_type=F32))
        tc = _col(twc_ref[...], lane, j1)
        ts = _col(tws_ref[...], lane, j1)
        o_ref[t, :, 0:FN_W] = (ar * tc + ai * ts).astype(o_ref.dtype)
        o_ref[t, :, FN_W:2 * FN_W] = (ai * tc - ar * ts).astype(o_ref.dtype)


def _fft_stage2_kernel(b_ref, c_ref, s_ref, w_ref, o_ref, *, k2_per_step):
    c = c_ref[...]
    s = s_ref[...]
    for t in range(k2_per_step):
        br = b_ref[:, t * 2 * FN_W:t * 2 * FN_W + FN_W]
        bi = b_ref[:, t * 2 * FN_W + FN_W:(t + 1) * 2 * FN_W]
        xr = jnp.dot(c, br, preferred_element_type=F32) + jnp.dot(s, bi, preferred_element_type=F32)
        o_ref[:, t * FN_W:(t + 1) * FN_W] = _dot(xr, w_ref[...]).astype(o_ref.dtype)


def _fnet_pos_fft(y, tabs, fnet_w, layer, *, batch, seq_len):
    c2, s2, twc, tws, c1, s1 = tabs
    l1 = seq_len // HEAD
    per = min(8, l1)
    yv = y.reshape(y.shape[0] // l1, l1 * 2 * FN_W)
    b_arr = pl.pallas_call(
        functools.partial(_fft_stage1_kernel, j1_per_step=per),
        grid=(batch, l1 // per),
        in_specs=[pl.BlockSpec((HEAD, per * 2 * FN_W), lambda b, j: (b, j)),
                  pl.BlockSpec((HEAD, HEAD), lambda b, j: (0, 0)),
                  pl.BlockSpec((HEAD, HEAD), lambda b, j: (0, 0)),
                  pl.BlockSpec((HEAD, HEAD), lambda b, j: (0, 0)),
                  pl.BlockSpec((HEAD, HEAD), lambda b, j: (0, 0))],
        out_specs=pl.BlockSpec((per, HEAD, 2 * FN_W), lambda b, j: (b * (l1 // per) + j, 0, 0)),
        out_shape=jax.ShapeDtypeStruct((batch * l1, HEAD, 2 * FN_W), MXU_DT),
        compiler_params=_cp(("parallel", "parallel")),
        name="fnet_fft_stage1",
    )(yv, c2, s2, twc, tws)
    bv = b_arr.reshape(batch, l1, HEAD * 2 * FN_W)
    out = pl.pallas_call(
        functools.partial(_fft_stage2_kernel, k2_per_step=8),
        grid=(batch, HEAD // 8),
        in_specs=[pl.BlockSpec((None, l1, 8 * 2 * FN_W), lambda b, k: (b, 0, k)),
                  pl.BlockSpec((l1, l1), lambda b, k: (0, 0)),
                  pl.BlockSpec((l1, l1), lambda b, k: (0, 0)),
                  pl.BlockSpec((None, FN_W, FN_W), lambda b, k: (layer, 0, 0))],
        out_specs=pl.BlockSpec((None, l1, 8 * FN_W), lambda b, k: (b, 0, k)),
        out_shape=jax.ShapeDtypeStruct((batch, l1, HEAD * FN_W), MXU_DT),
        compiler_params=_cp(("parallel", "parallel")),
        name="fnet_fft_stage2",
    )(bv, c1, s1, fnet_w)
    return out.reshape(batch * seq_len, FN_W)


def _fft_tables(seq_len):
    l1 = seq_len // HEAD
    c2, s2 = _dft_tables(HEAD)
    c1, s1 = _dft_tables(l1)
    k2 = jnp.arange(HEAD, dtype=jnp.int32)[:, None]
    j1 = jnp.arange(HEAD, dtype=jnp.int32)[None, :]
    ang = ((k2 * j1) % seq_len).astype(F32) * (2.0 * math.pi / seq_len)
    return (c2.astype(MXU_DT), s2.astype(MXU_DT), jnp.cos(ang), jnp.sin(ang), c1.astype(MXU_DT), s1.astype(MXU_DT))


def _out_proj_kernel(x_ref, of_ref, ob_ref, gate_ref, nw_ref, at_ref, fn_ref, w_ref, mod_ref, g_ref, b_ref, o_ref,
                     dn_ref, *, alpha):
    acc = jnp.dot(at_ref[...], w_ref[DN_W:DN_W + AQ_W, :], preferred_element_type=F32)
    acc += _dot(fn_ref[...], w_ref[DN_W + AQ_W:, :])
    _dn_gated_norm(of_ref, ob_ref, gate_ref, nw_ref, dn_ref)
    acc += jnp.dot(dn_ref[...], w_ref[0:DN_W, :], preferred_element_type=F32)
    y = alpha * x_ref[...] + mod_ref[2:3, :] * acc
    o_ref[...] = _ln(y) * g_ref[0:1, :] + b_ref[0:1, :]


def _out_proj(x, o_f, o_b, p, norm_w, a_at, a_fn, w_out, mod, g, b, layer, *, seg_of_tile, n_rows, alpha, batch,
              blocks_per_seq):
    d = x.shape[1]
    tm = RB
    seg = lambda i: seg_of_tile(i // (TM // tm))
    n_lat = batch * blocks_per_seq
    nchunk = RB // CHUNK
    bidx = lambda i: jnp.where(i < n_lat, i // blocks_per_seq, i - n_lat)
    pos_f = lambda i: jnp.where(i < n_lat, 1 + i % blocks_per_seq, 0)
    pos_b = lambda i: jnp.where(i < n_lat, blocks_per_seq - i % blocks_per_seq, 0)
    return pl.pallas_call(
        functools.partial(_out_proj_kernel, alpha=alpha),
        grid=(n_rows // tm,),
        in_specs=[pl.BlockSpec((tm, d), lambda i: (i, 0)),
                  pl.BlockSpec((DN_HEADS, nchunk, CHUNK, HEAD), lambda i: (bidx(i), pos_f(i), 0, 0)),
                  pl.BlockSpec((DN_HEADS, nchunk, CHUNK, HEAD), lambda i: (bidx(i), pos_b(i), 0, 0)),
                  pl.BlockSpec((RB, DN_W), lambda i: (i, C_GATE // DN_W)),
                  pl.BlockSpec((8, HEAD), lambda i: (0, 0)),
                  pl.BlockSpec((tm, AQ_W), lambda i: (i, 0)),
                  pl.BlockSpec((tm, FN_W), lambda i: (i, 0)),
                  pl.BlockSpec((None, DN_W + AQ_W + FN_W, d), lambda i: (layer, 0, 0)),
                  pl.BlockSpec((None, 6, d), lambda i: (seg(i), 0, 0)),
                  pl.BlockSpec((None, 1, d), lambda i: (layer, 0, 0)),
                  pl.BlockSpec((None, 1, d), lambda i: (layer, 0, 0))],
        out_specs=pl.BlockSpec((tm, d), lambda i: (i, 0)),
        out_shape=jax.ShapeDtypeStruct((n_rows, d), F32),
        scratch_shapes=[pltpu.VMEM((RB, DN_W), MXU_DT)],
        compiler_params=_cp(("parallel",)),
        name="out_proj",
    )(x, o_f, o_b, p, norm_w, a_at, a_fn, w_out, mod, g, b)


def _ffn_kernel(x_ref, mod_ref, wg_ref, wu_ref, wd_ref, g_ref, b_ref, o_ref, h_ref, acc_ref, *, alpha):
    f = pl.program_id(1)

    @pl.when(f == 0)
    def _():
        h = _ln(x_ref[...]) * (1.0 + mod_ref[4:5, :]) + mod_ref[3:4, :]
        h_ref[...] = h.astype(MXU_DT)
        acc_ref[...] = jnp.zeros_like(acc_ref)

    h = h_ref[...]
    gate = jnp.dot(h, wg_ref[...], preferred_element_type=F32)
    up = jnp.dot(h, wu_ref[...], preferred_element_type=F32)
    acc_ref[...] += jnp.dot((_silu(gate) * up).astype(MXU_DT), wd_ref[...], preferred_element_type=F32)

    @pl.when(f == pl.num_programs(1) - 1)
    def _():
        y = alpha * x_ref[...] + mod_ref[5:6, :] * acc_ref[...]
        o_ref[...] = _ln(y) * g_ref[0:1, :] + b_ref[0:1, :]


def _ffn(x, mod, wg, wu, wd, g, b, layer, li, *, seg_of_tile, n_rows, alpha, tf):
    d = x.shape[1]
    ff = wg.shape[2]
    return pl.pallas_call(
        functools.partial(_ffn_kernel, alpha=alpha),
        grid=(n_rows // TM, ff // tf),
        in_specs=[pl.BlockSpec((TM, d), lambda i, f: (i, 0)),
                  pl.BlockSpec((None, 6, d), lambda i, f: (seg_of_tile(i), 0, 0)),
                  pl.BlockSpec((None, d, tf), lambda i, f: (li, 0, f)),
                  pl.BlockSpec((None, d, tf), lambda i, f: (li, 0, f)),
                  pl.BlockSpec((None, tf, d), lambda i, f: (li, f, 0)),
                  pl.BlockSpec((None, 1, d), lambda i, f: (layer, 0, 0)),
                  pl.BlockSpec((None, 1, d), lambda i, f: (layer, 0, 0))],
        out_specs=pl.BlockSpec((TM, d), lambda i, f: (i, 0)),
        out_shape=jax.ShapeDtypeStruct((n_rows, d), F32),
        scratch_shapes=[pltpu.VMEM((TM, d), MXU_DT), pltpu.VMEM((TM, d), F32)],
        compiler_params=_cp(("parallel", "arbitrary")),
        name="ffn",
    )(x, mod, wg, wu, wd, g, b)


def _router_kernel(x_ref, mod_ref, rw_ref, h_ref, idx_ref, gate_ref):
    h = _ln(x_ref[...]) * (1.0 + mod_ref[4:5, :]) + mod_ref[3:4, :]
    h_ref[...] = h.astype(h_ref.dtype)
    logits = jnp.dot(h, rw_ref[...], precision=HI, preferred_element_type=F32)
    lane = lax.broadcasted_iota(jnp.int32, logits.shape, 1)
    neg = jnp.float32(-jnp.inf)
    l1 = jnp.where(lane < N_EXPERTS, logits, neg)
    m1 = jnp.max(l1, axis=-1, keepdims=True)
    i1 = jnp.min(jnp.where(l1 == m1, lane, HEAD), axis=-1, keepdims=True)
    l2 = jnp.where(lane == i1, neg, l1)
    m2 = jnp.max(l2, axis=-1, keepdims=True)
    i2 = jnp.min(jnp.where(l2 == m2, lane, HEAD), axis=-1, keepdims=True)
    e = jnp.exp(m2 - m1)
    g1 = 1.0 / (1.0 + e)
    g2 = e / (1.0 + e)
    idx_ref[...] = jnp.where(lane == 0, i1, jnp.where(lane == 1, i2, 0))
    gate_ref[...] = jnp.where(lane == 0, g1, jnp.where(lane == 1, g2, 0.0))


def _router(x, mod, rw, *, seg_of_tile, n_rows):
    d = x.shape[1]
    return pl.pallas_call(
        _router_kernel,
        grid=(n_rows // TM,),
        in_specs=[pl.BlockSpec((TM, d), lambda i: (i, 0)),
                  pl.BlockSpec((None, 6, d), lambda i: (seg_of_tile(i), 0, 0)),
                  pl.BlockSpec((d, HEAD), lambda i: (0, 0))],
        out_specs=[pl.BlockSpec((TM, d), lambda i: (i, 0)),
                   pl.BlockSpec((TM, HEAD), lambda i: (i, 0)),
                   pl.BlockSpec((TM, HEAD), lambda i: (i, 0))],
        out_shape=[jax.ShapeDtypeStruct((n_rows, d), MXU_DT),
                   jax.ShapeDtypeStruct((n_rows, HEAD), jnp.int32),
                   jax.ShapeDtypeStruct((n_rows, HEAD), F32)],
        compiler_params=_cp(("parallel",)),
        name="router",
    )(x, mod, rw)


def _moe_kernel(te_ref, nv_ref, h_ref, wg_ref, wu_ref, wd_ref, o_ref, acc_ref):
    t = pl.program_id(0)
    f = pl.program_id(1)
    last = pl.num_programs(1) - 1
    valid = t < nv_ref[0]

    @pl.when(jnp.logical_and(valid, f == 0))
    def _():
        acc_ref[...] = jnp.zeros_like(acc_ref)

    @pl.when(valid)
    def _():
        h = h_ref[...]
        gate = _dot(h, wg_ref[...])
        up = _dot(h, wu_ref[...])
        acc_ref[...] += _dot(_silu(gate) * up, wd_ref[...])

    @pl.when(jnp.logical_and(valid, f == last))
    def _():
        o_ref[...] = acc_ref[...].astype(o_ref.dtype)

    @pl.when(jnp.logical_and(jnp.logical_not(valid), f == last))
    def _():
        o_ref[...] = jnp.zeros_like(o_ref)


def _moe_experts(tile_expert, n_valid, h_sorted, wg, wu, wd, mi, *, tf):
    mp, d = h_sorted.shape
    ff = wg.shape[3]
    nf = ff // tf

    def fidx(t, f, nv):
        return jnp.where(t < nv[0], f, nf - 1)

    grid_spec = pltpu.PrefetchScalarGridSpec(
        num_scalar_prefetch=2,
        grid=(mp // TM_MOE, nf),
        in_specs=[pl.BlockSpec((TM_MOE, d), lambda t, f, te, nv: (t, 0)),
                  pl.BlockSpec((None, None, d, tf), lambda t, f, te, nv: (mi, te[t], 0, fidx(t, f, nv))),
                  pl.BlockSpec((None, None, d, tf), lambda t, f, te, nv: (mi, te[t], 0, fidx(t, f, nv))),
                  pl.BlockSpec((None, None, tf, d), lambda t, f, te, nv: (mi, te[t], fidx(t, f, nv), 0))],
        out_specs=pl.BlockSpec((TM_MOE, d), lambda t, f, te, nv: (t, 0)),
        scratch_shapes=[pltpu.VMEM((TM_MOE, d), F32)],
    )
    return pl.pallas_call(
        _moe_kernel,
        grid_spec=grid_spec,
        out_shape=jax.ShapeDtypeStruct((mp, d), MXU_DT),
        compiler_params=_cp(("arbitrary", "arbitrary"), vmem=V7X_VMEM_LIMIT_MOE),
        name="moe_experts",
    )(tile_expert, n_valid, h_sorted, wg, wu, wd)


def _moe_combine_kernel(x_ref, y1_ref, y2_ref, gate_ref, mod_ref, g_ref, b_ref, o_ref, *, alpha):
    gates = gate_ref[...]
    y = gates[:, 0:1] * y1_ref[...].astype(F32) + gates[:, 1:2] * y2_ref[...].astype(F32)
    z = alpha * x_ref[...] + mod_ref[5:6, :] * y
    o_ref[...] = _ln(z) * g_ref[0:1, :] + b_ref[0:1, :]


def _moe_combine(x, y12, gates, mod, g, b, layer, *, seg_of_tile, n_rows, alpha):
    d = x.shape[1]
    tm = RB
    nt = n_rows // tm
    seg = lambda i: seg_of_tile(i // (TM // tm))
    return pl.pallas_call(
        functools.partial(_moe_combine_kernel, alpha=alpha),
        grid=(nt,),
        in_specs=[pl.BlockSpec((tm, d), lambda i: (i, 0)),
                  pl.BlockSpec((tm, d), lambda i: (i, 0)),
                  pl.BlockSpec((tm, d), lambda i: (nt + i, 0)),
                  pl.BlockSpec((tm, HEAD), lambda i: (i, 0)),
                  pl.BlockSpec((None, 6, d), lambda i: (seg(i), 0, 0)),
                  pl.BlockSpec((None, 1, d), lambda i: (layer, 0, 0)),
                  pl.BlockSpec((None, 1, d), lambda i: (layer, 0, 0))],
        out_specs=pl.BlockSpec((tm, d), lambda i: (i, 0)),
        out_shape=jax.ShapeDtypeStruct((n_rows, d), F32),
        compiler_params=_cp(("parallel",)),
        name="moe_combine",
    )(x, y12, y12, gates, mod, g, b)


def _moe_layer(x, mod, rw, wg, wu, wd, g, b, layer, mi, *, seg_of_tile, n_rows, alpha, tf):
    h, idx, gates = _router(x, mod, rw, seg_of_tile=seg_of_tile, n_rows=n_rows)
    na = 2 * n_rows
    mp = na + N_EXPERTS * TM_MOE
    e_flat = idx[:, 0:2].reshape(na)
    onehot = (e_flat[:, None] == jnp.arange(N_EXPERTS, dtype=jnp.int32)[None, :]).astype(jnp.int32)
    csum = jnp.cumsum(onehot, axis=0)
    rank = jnp.sum((csum - onehot) * onehot, axis=1)
    counts = csum[-1]
    padded = ((counts + TM_MOE - 1) // TM_MOE) * TM_MOE
    ends = jnp.cumsum(padded)
    dest = (ends - padded)[e_flat] + rank
    src_token = (jnp.arange(mp, dtype=jnp.int32) % n_rows).at[dest].set(jnp.arange(na, dtype=jnp.int32) // 2)
    tile_start = jnp.arange(mp // TM_MOE, dtype=jnp.int32) * TM_MOE
    tile_expert = jnp.sum((ends[None, :] <= tile_start[:, None]).astype(jnp.int32), axis=1)
    tile_expert = jnp.minimum(tile_expert, N_EXPERTS - 1)
    n_valid = (ends[-1] // TM_MOE).astype(jnp.int32).reshape(1)
    h_sorted = jnp.take(h, src_token, axis=0, mode="clip")
    y_sorted = _moe_experts(tile_expert, n_valid, h_sorted, wg, wu, wd, mi, tf=tf)
    y12 = jnp.take(y_sorted, dest.reshape(n_rows, 2).T.reshape(na), axis=0, mode="clip")
    return _moe_combine(x, y12, gates, mod, g, b, layer, seg_of_tile=seg_of_tile, n_rows=n_rows, alpha=alpha)


def _rope_tables(seq_len):
    axis_dim = HEAD // 2
    inv = ROPE_THETA ** (-jnp.arange(0, axis_dim, 2, dtype=F32) / axis_dim)
    t = jnp.arange(seq_len, dtype=jnp.int32)
    row = (t // GRID_W).astype(F32)[:, None] * inv
    col = (t % GRID_W).astype(F32)[:, None] * inv
    cos = jnp.concatenate([jnp.cos(row), jnp.cos(row), jnp.cos(col), jnp.cos(col)], axis=-1)
    sin = jnp.concatenate([-jnp.sin(row), jnp.sin(row), -jnp.sin(col), jnp.sin(col)], axis=-1)
    return cos, sin


def _pad_rows8(v):
    return jnp.zeros((8, v.shape[-1]), F32).at[0].set(v.astype(F32))


def kernel(x, c, ctx, c_ctx, w_mod, b_mod, w_in, dn_conv, dn_a_log, dn_dt_bias, dn_norm, attn_q_norm, attn_k_norm,
           fnet_w, w_out, ln1_g, ln1_b, ln2_g, ln2_b, ffn_w_gate, ffn_w_up, ffn_w_down, router, moe_w_gate,
           moe_w_up, moe_w_down):
    batch, seq, d = x.shape
    ctx_len = ctx.shape[1]
    depth = w_mod.shape[0]
    ff = ffn_w_gate.shape[2]
    assert ctx_len == RB and seq % TM == 0 and (batch * ctx_len) % TM == 0 and TM % RB == 0
    alpha = (2 * depth) ** 0.25
    n_lat = batch * seq
    n_all = n_lat + batch * ctx_len
    tiles_per_seq = seq // TM
    blocks_per_seq = seq // RB
    seg_of_tile = lambda i: jnp.minimum(i // tiles_per_seq, batch)
    tf = 512 if ff % 512 == 0 else ff

    w_in_packed = jnp.concatenate(
        [w_in[:, :, :4 * DN_W], w_in[:, :, 4 * DN_W + 4 * DN_HEADS:], w_in[:, :, 4 * DN_W:4 * DN_W + 4 * DN_HEADS],
         jnp.zeros(w_in.shape[:2] + (HEAD - 4 * DN_HEADS,), w_in.dtype)], axis=-1).astype(MXU_DT)
    conv_w = jnp.pad(dn_conv, ((0, 0), (0, 8 - CONV_K), (0, 0)))
    w_out_c = w_out.astype(MXU_DT)
    fnet_w_c = fnet_w.astype(MXU_DT)
    ffn_wg, ffn_wu, ffn_wd = ffn_w_gate.astype(MXU_DT), ffn_w_up.astype(MXU_DT), ffn_w_down.astype(MXU_DT)
    router_p = jnp.pad(router, ((0, 0), (0, 0), (0, HEAD - N_EXPERTS)))
    b_mod3 = b_mod[:, None, :]
    ln1_g3, ln1_b3, ln2_g3, ln2_b3 = ln1_g[:, None, :], ln1_b[:, None, :], ln2_g[:, None, :], ln2_b[:, None, :]

    rope_cos, rope_sin = _rope_tables(seq)
    lat_tabs = _fft_tables(seq)
    ctx_cos, ctx_sin = (t.astype(MXU_DT) for t in _dft_tables(ctx_len))
    ch_cos, ch_sin = _dft_tables(HEAD)
    eye_g = jnp.eye(FN_GROUPS, dtype=F32)
    ch_cs = jnp.concatenate([jnp.kron(eye_g, ch_cos), jnp.kron(eye_g, ch_sin)], axis=-1).astype(MXU_DT)

    c8 = jnp.zeros((8, d), F32).at[0:batch].set(c).at[batch].set(c_ctx)
    xs = jnp.concatenate([x.reshape(n_lat, d), ctx.reshape(batch * ctx_len, d)], axis=0)

    mod_all = _modulation(c8, w_mod, b_mod3).reshape(depth, 8, 6, d)

    for layer in range(depth):
        need_ctx = layer < depth - 1
        rows_out = n_all if need_ctx else n_lat
        mod = mod_all[layer]

        p = _ln_mod_matmul(xs, mod, w_in_packed, layer, seg_of_tile=seg_of_tile, shift_row=0,
                           tn=(P_COLS + HEAD) // 3)

        par = (jnp.zeros((8, HEAD), F32)
               .at[0, 2 * DN_HEADS:4 * DN_HEADS].set(dn_dt_bias[layer].reshape(-1))
               .at[1, 2 * DN_HEADS:4 * DN_HEADS].set(dn_a_log[layer].reshape(-1)))
        prep = _dn_prep(p, conv_w, layer, par, batch=batch, blocks_per_seq=blocks_per_seq)
        o_f, o_b = _dn_scan(prep)

        qh, kh, vh = _attn_prep(p, rope_cos, rope_sin, _pad_rows8(attn_q_norm[layer]), _pad_rows8(attn_k_norm[layer]),
                                batch=batch, blocks_per_seq=blocks_per_seq)
        a_at = _attention(qh, kh, vh, lat_len=seq, with_ctx=need_ctx, n_rows=rows_out)

        y_lat = _fnet_channel_dft(p, ch_cs, row_off=0, n_rows=n_lat)
        a_fn = _fnet_pos_fft(y_lat, lat_tabs, fnet_w_c, layer, batch=batch, seq_len=seq)
        if need_ctx:
            y_ctx = _fnet_channel_dft(p, ch_cs, row_off=n_lat, n_rows=batch * ctx_len)
            a_fn_c = _fnet_pos_dft(y_ctx, ctx_cos, ctx_sin, fnet_w_c, layer, batch=batch, seq_len=ctx_len, row_off=0)
            a_fn = jnp.concatenate([a_fn, a_fn_c.reshape(batch * ctx_len, FN_W)], axis=0)

        xs = _out_proj(xs, o_f, o_b, p, _pad_rows8(dn_norm[layer]), a_at, a_fn, w_out_c, mod, ln1_g3, ln1_b3, layer,
                       seg_of_tile=seg_of_tile, n_rows=rows_out, alpha=alpha, batch=batch,
                       blocks_per_seq=blocks_per_seq)

        li = layer // 2
        if layer % 2 == 0:
            xs = _ffn(xs, mod, ffn_wg, ffn_wu, ffn_wd, ln2_g3, ln2_b3, layer, li, seg_of_tile=seg_of_tile,
                      n_rows=rows_out, alpha=alpha, tf=tf)
        else:
            xs = _moe_layer(xs, mod, router_p[li], moe_w_gate, moe_w_up, moe_w_down, ln2_g3, ln2_b3, layer, li,
                            seg_of_tile=seg_of_tile, n_rows=rows_out, alpha=alpha,
                            tf=TF_MOE if ff % TF_MOE == 0 else ff)
    return xs[:n_lat].reshape(batch, seq, d)
```
